```python
import math
import jax, jax.numpy as jnp
from jax import lax
import numpy as np

D_MODEL = 1024
BATCH = 8
SEQ = 2048
DEPTH = 2

HEAD_DIM = 64
ROPE_THETA = 10000.0
NORM_EPS = 1e-6
NEG_INF = -1e30
Q_BLOCK = 128

A_HEADS = 8
A_TOPK_MAX = 256
IDX_HEADS = 8
IDX_DIM = HEAD_DIM
B_HEADS = 4
B_VDIM = 2 * HEAD_DIM
SUBLN_EPS = 1e-5
C_HEADS = D_MODEL // HEAD_DIM
C_PATTERNS = ((128, 1), (512, 4), (2048, 16))
D_FF = ((8 * D_MODEL // 3 + 127) // 128) * 128
PLE_DIM = 256

N_EVEN = (DEPTH + 1) // 2
N_ODD = DEPTH // 2

EVEN_SPLITS = (
    A_HEADS * HEAD_DIM,
    HEAD_DIM,
    HEAD_DIM,
    IDX_HEADS * IDX_DIM,
    IDX_DIM,
    IDX_HEADS,
    2 * B_HEADS * HEAD_DIM,
    2 * B_HEADS * HEAD_DIM,
    B_HEADS * B_VDIM,
)
EVEN_IN = sum(EVEN_SPLITS)
EVEN_MIX_WIDTH = A_HEADS * HEAD_DIM + B_HEADS * B_VDIM

kernel_name = "hybrid_dsa_diff_dilated_macaron"

f32 = jnp.float32


def rms_norm(x, g, eps=NORM_EPS):
    x32 = x.astype(f32)
    y = x32 * lax.rsqrt(jnp.mean(x32 * x32, axis=-1, keepdims=True) + eps)
    return (y * g.astype(f32)).astype(x.dtype)


def rope_tables(T, dim):
    inv = 1.0 / (ROPE_THETA ** (jnp.arange(0, dim, 2, dtype=f32) / dim))
    ang = jnp.arange(T, dtype=f32)[:, None] * inv[None, :]
    return jnp.cos(ang), jnp.sin(ang)


def apply_rope(x, cos, sin):
    shape = (x.shape[1],) + (1,) * (x.ndim - 3) + (cos.shape[-1],)
    c, s = cos.reshape(shape), sin.reshape(shape)
    x1, x2 = jnp.split(x.astype(f32), 2, axis=-1)
    return jnp.concatenate([x1 * c - x2 * s, x2 * c + x1 * s], axis=-1).astype(x.dtype)


def swiglu(h, wg, wu, wd):
    return (jax.nn.silu(h @ wg) * (h @ wu)) @ wd


def to_query_blocks(a):
    B, T = a.shape[:2]
    a = a.reshape((B, T // Q_BLOCK, Q_BLOCK) + a.shape[2:])
    return jnp.moveaxis(a, 1, 0)


def from_query_blocks(a):
    a = jnp.moveaxis(a, 0, 1)
    return a.reshape((a.shape[0], a.shape[1] * a.shape[2]) + a.shape[3:])


def dsa_attention(q, k, v, q_idx, k_idx, w_idx, top_k):
    T = q.shape[1]
    scale = HEAD_DIM ** -0.5
    idx_scale = (IDX_DIM ** -0.5) * (IDX_HEADS ** -0.5)
    k_idx32 = k_idx.astype(f32)
    key_pos = jnp.arange(T)
    gather = jax.vmap(lambda a, i: a[i])

    def block(args):
        qb, qib, wb, start = args
        qpos = start + jnp.arange(Q_BLOCK)
        causal = key_pos[None, :] <= qpos[:, None]
        rel = jax.nn.relu(jnp.einsum('bqhd,bsd->bqhs', qib.astype(f32), k_idx32))
        iscore = jnp.einsum('bqhs,bqh->bqs', rel, wb.astype(f32)) * idx_scale
        iscore = jnp.where(causal[None], iscore, NEG_INF)
        _, sel = lax.top_k(iscore, top_k)
        ks = gather(k, sel)
        vs = gather(v, sel)
        s = jnp.einsum('bqhd,bqkd->bqhk', qb, ks).astype(f32) * scale
        valid = (sel <= qpos[None, :, None])[:, :, None, :]
        pr = jax.nn.softmax(jnp.where(valid, s, NEG_INF), axis=-1)
        return jnp.einsum('bqhk,bqkd->bqhd', pr.astype(vs.dtype), vs)

    starts = jnp.arange(T // Q_BLOCK) * Q_BLOCK
    out = lax.map(block, (to_query_blocks(q), to_query_blocks(q_idx), to_query_blocks(w_idx), starts))
    return from_query_blocks(out)


def diff_attention(q1, q2, k1, k2, v, lam):
    T = q1.shape[1]
    scale = HEAD_DIM ** -0.5
    key_pos = jnp.arange(T)

    def block(args):
        q1b, q2b, start = args
        qpos = start + jnp.arange(Q_BLOCK)
        causal = (key_pos[None, :] <= qpos[:, None])[None, None]

        def probs(qb, kk):
            s = jnp.einsum('bqhd,bshd->bhqs', qb, kk).astype(f32) * scale
            return jax.nn.softmax(jnp.where(causal, s, NEG_INF), axis=-1)

        a = probs(q1b, k1) - lam * probs(q2b, k2)
        return jnp.einsum('bhqs,bshe->bqhe', a.astype(v.dtype), v)

    starts = jnp.arange(T // Q_BLOCK) * Q_BLOCK
    out = lax.map(block, (to_query_blocks(q1), to_query_blocks(q2), starts))
    return from_query_blocks(out)


def dilated_branch(q, k, v, window, dilation):
    B, T, H, D = q.shape
    span = window // dilation
    n = T // dilation
    nb = -(-n // span)
    n_pad = nb * span
    Z = B * dilation

    def to_sub(a):
        E = a.shape[-1]
        a = a.reshape(B, n, dilation, H, E).transpose(0, 2, 1, 3, 4).reshape(Z, n, H, E)
        return jnp.pad(a, ((0, 0), (0, n_pad - n), (0, 0), (0, 0)))

    def banded(a):
        E = a.shape[-1]
        a = jnp.pad(a, ((0, 0), (span, 0), (0, 0), (0, 0))).reshape(Z, nb + 1, span, H, E)
        return jnp.concatenate([a[:, :-1], a[:, 1:]], axis=2)

    qb = to_sub(q).reshape(Z, nb, span, H, D)
    kb = banded(to_sub(k))
    vb = banded(to_sub(v))
    dist = jnp.arange(span)[:, None] + span - jnp.arange(2 * span)[None, :]
    m_key = jnp.arange(nb)[:, None] * span + jnp.arange(2 * span)[None, :] - span
    mask = ((dist >= 0) & (dist <= span))[None] & ((m_key >= 0) & (m_key < n))[:, None, :]
    s = jnp.einsum('znqhd,znkhd->znhqk', qb, kb).astype(f32) * (HEAD_DIM ** -0.5)
    s = jnp.where(mask[None, :, None], s, NEG_INF)
    lse = jax.nn.logsumexp(s, axis=-1)
    pr = jnp.exp(s - lse[..., None])
    o = jnp.einsum('znhqk,znkhd->znqhd', pr.astype(v.dtype), vb)

    def from_sub(a):
        a = a.reshape((B, dilation, n_pad) + a.shape[3:])[:, :, :n]
        a = jnp.moveaxis(a, 1, 2)
        return a.reshape((B, T) + a.shape[3:])

    return from_sub(o), from_sub(jnp.moveaxis(lse, 2, 3))


def dilated_mixture(q, k, v):
    outs, lses = [], []
    for window, dilation in C_PATTERNS:
        o, l = dilated_branch(q, k, v, window, dilation)
        outs.append(o)
        lses.append(l)
    alpha = jax.nn.softmax(jnp.stack(lses, 0), axis=0)
    return jnp.sum(alpha[..., None].astype(q.dtype) * jnp.stack(outs, 0), axis=0)


def even_mixer(h, w_in, w_out, lq1, lk1, lq2, lk2, subln, lambda_init, cos, sin, top_k):
    B, T, _ = h.shape
    cuts = [int(c) for c in np.cumsum(EVEN_SPLITS)[:-1]]
    qa, ka, va, qi, ki, wi, qb, kb, vb = jnp.split(h @ w_in, cuts, axis=-1)
    qa = apply_rope(qa.reshape(B, T, A_HEADS, HEAD_DIM), cos, sin)
    ka = apply_rope(ka, cos, sin)
    qi = apply_rope(qi.reshape(B, T, IDX_HEADS, IDX_DIM), cos, sin)
    ki = apply_rope(ki, cos, sin)
    out_a = dsa_attention(qa, ka, va, qi, ki, wi, top_k).reshape(B, T, A_HEADS * HEAD_DIM)
    qb = apply_rope(qb.reshape(B, T, 2 * B_HEADS, HEAD_DIM), cos, sin).reshape(B, T, B_HEADS, 2, HEAD_DIM)
    kb = apply_rope(kb.reshape(B, T, 2 * B_HEADS, HEAD_DIM), cos, sin).reshape(B, T, B_HEADS, 2, HEAD_DIM)
    vb = vb.reshape(B, T, B_HEADS, B_VDIM)
    lam = (jnp.exp(jnp.sum(lq1.astype(f32) * lk1.astype(f32)))
           - jnp.exp(jnp.sum(lq2.astype(f32) * lk2.astype(f32))) + lambda_init)
    ob = diff_attention(qb[..., 0, :], qb[..., 1, :], kb[..., 0, :], kb[..., 1, :], vb, lam)
    ob = rms_norm(ob, subln, SUBLN_EPS) * (1.0 - lambda_init)
    merged = jnp.concatenate([out_a, ob.reshape(B, T, B_HEADS * B_VDIM)], axis=-1)
    return merged @ w_out


def odd_mixer(h, w_in, w_out, cos, sin):
    B, T, _ = h.shape
    q, k, v = jnp.split(h @ w_in, 3, axis=-1)
    q = apply_rope(q.reshape(B, T, C_HEADS, HEAD_DIM), cos, sin)
    k = apply_rope(k.reshape(B, T, C_HEADS, HEAD_DIM), cos, sin)
    v = v.reshape(B, T, C_HEADS, HEAD_DIM)
    return dilated_mixture(q, k, v).reshape(B, T, C_HEADS * HEAD_DIM) @ w_out


def setup_inputs(seed: int = 0) -> dict:
    key = jax.random.key(seed)
    ks = iter(jax.random.split(key, 32))

    def w(shape, fan_in):
        return jax.random.normal(next(ks), shape, f32) * (fan_in ** -0.5)

    def gain(shape):
        return 1.0 + 0.02 * jax.random.normal(next(ks), shape, f32)

    return {
        "x": jax.random.normal(next(ks), (BATCH, SEQ, D_MODEL), f32),
        "p": jax.random.normal(next(ks), (DEPTH, BATCH, SEQ, PLE_DIM), f32),
        "norm_ffn_a": gain((DEPTH, D_MODEL)),
        "ffn_a_wg": w((DEPTH, D_MODEL, D_FF), D_MODEL),
        "ffn_a_wu": w((DEPTH, D_MODEL, D_FF), D_MODEL),
        "ffn_a_wd": w((DEPTH, D_FF, D_MODEL), D_FF),
        "norm_mix": gain((DEPTH, D_MODEL)),
        "norm_ffn_b": gain((DEPTH, D_MODEL)),
        "ffn_b_wg": w((DEPTH, D_MODEL, D_FF), D_MODEL),
        "ffn_b_wu": w((DEPTH, D_MODEL, D_FF), D_MODEL),
        "ffn_b_wd": w((DEPTH, D_FF, D_MODEL), D_FF),
        "norm_ple": gain((DEPTH, D_MODEL)),
        "ple_gate": w((DEPTH, D_MODEL, D_MODEL), D_MODEL),
        "ple_proj": w((DEPTH, PLE_DIM, D_MODEL), PLE_DIM),
        "even_w_in": w((N_EVEN, D_MODEL, EVEN_IN), D_MODEL),
        "even_w_out": w((N_EVEN, EVEN_MIX_WIDTH, D_MODEL), EVEN_MIX_WIDTH),
        "diff_lambda_q1": 0.1 * jax.random.normal(next(ks), (N_EVEN, HEAD_DIM), f32),
        "diff_lambda_k1": 0.1 * jax.random.normal(next(ks), (N_EVEN, HEAD_DIM), f32),
        "diff_lambda_q2": 0.1 * jax.random.normal(next(ks), (N_EVEN, HEAD_DIM), f32),
        "diff_lambda_k2": 0.1 * jax.random.normal(next(ks), (N_EVEN, HEAD_DIM), f32),
        "diff_subln": gain((N_EVEN, B_VDIM)),
        "odd_w_in": w((N_ODD, D_MODEL, 3 * C_HEADS * HEAD_DIM), D_MODEL),
        "odd_w_out": w((N_ODD, C_HEADS * HEAD_DIM, D_MODEL), C_HEADS * HEAD_DIM),
        "final_norm": gain((D_MODEL,)),
    }


def reference(x, p, norm_ffn_a, ffn_a_wg, ffn_a_wu, ffn_a_wd, norm_mix, norm_ffn_b,
              ffn_b_wg, ffn_b_wu, ffn_b_wd, norm_ple, ple_gate, ple_proj, even_w_in,
              even_w_out, diff_lambda_q1, diff_lambda_k1, diff_lambda_q2, diff_lambda_k2,
              diff_subln, odd_w_in, odd_w_out, final_norm):
    T = x.shape[1]
    top_k = min(A_TOPK_MAX, T // 4)
    cos, sin = rope_tables(T, HEAD_DIM)
    h = x
    for i in range(DEPTH):
        h = h + 0.5 * swiglu(rms_norm(h, norm_ffn_a[i]), ffn_a_wg[i], ffn_a_wu[i], ffn_a_wd[i])
        hn = rms_norm(h, norm_mix[i])
        if i % 2 == 0:
            e = i // 2
            lambda_init = 0.8 - 0.6 * math.exp(-0.3 * i)
            mix = even_mixer(hn, even_w_in[e], even_w_out[e], diff_lambda_q1[e], diff_lambda_k1[e],
                             diff_lambda_q2[e], diff_lambda_k2[e], diff_subln[e], lambda_init,
                             cos, sin, top_k)
        else:
            o = i // 2
            mix = odd_mixer(hn, odd_w_in[o], odd_w_out[o], cos, sin)
        h = h + mix
        h = h + 0.5 * swiglu(rms_norm(h, norm_ffn_b[i]), ffn_b_wg[i], ffn_b_wu[i], ffn_b_wd[i])
        gate = jax.nn.sigmoid(rms_norm(h, norm_ple[i]) @ ple_gate[i])
        h = h + gate * (p[i] @ ple_proj[i])
    return rms_norm(h, final_norm)
```

```python
import functools
import math

import jax
import jax.numpy as jnp
import numpy as np
from jax import lax
from jax.experimental import pallas as pl
from jax.experimental.pallas import tpu as pltpu

F32 = jnp.float32
BF16 = jnp.bfloat16
I32 = jnp.int32

HEAD_DIM = 64
ROPE_THETA = 10000.0
NORM_EPS = 1e-6
SUBLN_EPS = 1e-5
NEG_INF = -1e30
A_HEADS = 8
A_TOPK_MAX = 256
IDX_HEADS = 8
B_HEADS = 4
B_VDIM = 2 * HEAD_DIM
C_PATTERNS = ((128, 1), (512, 4), (2048, 16))
DIL_SPAN = 128

LANES = 128
VMEM_LIMIT_BYTES = 56 * 1024 * 1024
TOKEN_TILE = 256
FFN_CHUNKS = 2
ATTN_TQ = 256

INT_MIN = -(2 ** 31)
IDX_BIG = 2 ** 30

_CONTRACT_LAST = (((1,), (1,)), ((), ()))


def _cparams(n_grid):
    return pltpu.CompilerParams(dimension_semantics=("arbitrary",) * n_grid,
                                vmem_limit_bytes=VMEM_LIMIT_BYTES)


def _resident(shape):
    nd = len(shape)
    return pl.BlockSpec(shape, lambda *_: (0,) * nd, pipeline_mode=pl.Buffered(1))


def _rms(x, g, eps):
    ms = jnp.mean(x * x, axis=-1, keepdims=True)
    return x * lax.rsqrt(ms + eps) * g


def _ffn_half(h, g, wg_ref, wu_ref, wd_ref):
    xn = _rms(h, g, NORM_EPS).astype(BF16)
    d_ff = wg_ref.shape[1]
    fc = d_ff // FFN_CHUNKS
    acc = None
    for c in range(FFN_CHUNKS):
        a = jnp.dot(xn, wg_ref[:, c * fc:(c + 1) * fc], preferred_element_type=F32)
        b = jnp.dot(xn, wu_ref[:, c * fc:(c + 1) * fc], preferred_element_type=F32)
        mid = (a / (1.0 + jnp.exp(-a)) * b).astype(BF16)
        part = jnp.dot(mid, wd_ref[c * fc:(c + 1) * fc, :], preferred_element_type=F32)
        acc = part if acc is None else acc + part
    return h + 0.5 * acc


def _rope(y, cos_t, sin_t):
    w = y.shape[1]
    n = w // LANES
    c = jnp.concatenate([cos_t] * n, axis=1) if n > 1 else cos_t
    s = jnp.concatenate([sin_t] * n, axis=1) if n > 1 else sin_t
    lane = lax.broadcasted_iota(I32, y.shape, 1)
    first_half = (lane & (HEAD_DIM // 2)) == 0
    partner = jnp.where(first_half,
                        pltpu.roll(y, w - HEAD_DIM // 2, 1),
                        pltpu.roll(y, HEAD_DIM // 2, 1))
    return y * c + partner * s


def _pre_even_kernel(h_ref, ga_ref, wg_ref, wu_ref, wd_ref, gm_ref, win_ref, cos_ref, sin_ref,
                     ha_ref, qa_ref, qi_ref, qb_ref, kb_ref, ka_ref, ki_ref, va_ref, vb_ref, wi_ref,
                     *, idx_scale):
    ha = _ffn_half(h_ref[...], ga_ref[...], wg_ref, wu_ref, wd_ref)
    ha_ref[...] = ha
    xn = _rms(ha, gm_ref[...], NORM_EPS).astype(BF16)
    y = jnp.dot(xn, win_ref[...], preferred_element_type=F32)
    n_rope = 4 * 512 + 128
    yr = _rope(y[:, :n_rope], cos_ref[...], sin_ref[...])
    q_scale = HEAD_DIM ** -0.5
    for hd in range(8):
        lo = hd * HEAD_DIM
        qa_ref[hd] = (yr[:, lo:lo + HEAD_DIM] * q_scale).astype(BF16)
        qi_ref[hd] = yr[:, 512 + lo:512 + lo + HEAD_DIM].astype(BF16)
        qb_ref[hd] = (yr[:, 1024 + lo:1024 + lo + HEAD_DIM] * q_scale).astype(BF16)
        kb_ref[hd] = yr[:, 1536 + lo:1536 + lo + HEAD_DIM].astype(BF16)
    ka_ref[...] = yr[:, 2048:2112].astype(BF16)
    ki_ref[...] = yr[:, 2112:2176].astype(BF16)
    vb_ref[...] = y[:, 2176:2688].astype(BF16)
    va_ref[...] = y[:, 2688:2752].astype(BF16)
    wi_ref[...] = y[:, 2752:2760] * idx_scale


def _pre_odd_kernel(h_ref, ga_ref, wg_ref, wu_ref, wd_ref, gm_ref, win_ref, cos_ref, sin_ref,
                    ha_ref, q_ref, k_ref, v_ref):
    ha = _ffn_half(h_ref[...], ga_ref[...], wg_ref, wu_ref, wd_ref)
    ha_ref[...] = ha
    xn = _rms(ha, gm_ref[...], NORM_EPS).astype(BF16)
    y = jnp.dot(xn, win_ref[...], preferred_element_type=F32)
    d = q_ref.shape[1]
    yr = _rope(y[:, :2 * d], cos_ref[...], sin_ref[...])
    q_ref[...] = yr[:, :d] * (HEAD_DIM ** -0.5)
    k_ref[...] = yr[:, d:2 * d]
    v_ref[...] = y[:, 2 * d:3 * d]


def _token_specs(tm, d, seq_tiles):
    row = lambda w: pl.BlockSpec((tm, w), lambda i: (i, 0))
    rope = pl.BlockSpec((tm, LANES), lambda i: (i % seq_tiles, 0))
    return row, rope


def _pre_even(h, ga, wg, wu, wd, gm, win, cos_t, sin_t, *, batch, seq):
    m, d = h.shape
    tm = TOKEN_TILE
    nt = seq // tm
    row, rope = _token_specs(tm, d, nt)
    hm = lambda: pl.BlockSpec((None, 8, tm, HEAD_DIM), lambda i: (i // nt, 0, i % nt, 0))
    hm_shape = jax.ShapeDtypeStruct((batch, 8, seq, HEAD_DIM), BF16)
    idx_scale = (HEAD_DIM ** -0.5) * (IDX_HEADS ** -0.5)
    return pl.pallas_call(
        functools.partial(_pre_even_kernel, idx_scale=idx_scale),
        grid=(m // tm,),
        in_specs=[row(d), _resident(ga.shape), _resident(wg.shape), _resident(wu.shape),
                  _resident(wd.shape), _resident(gm.shape), _resident(win.shape), rope, rope],
        out_specs=[row(d), hm(), hm(), hm(), hm(), row(HEAD_DIM), row(HEAD_DIM), row(HEAD_DIM),
                   row(512), row(IDX_HEADS)],
        out_shape=[jax.ShapeDtypeStruct((m, d), F32), hm_shape, hm_shape, hm_shape, hm_shape,
                   jax.ShapeDtypeStruct((m, HEAD_DIM), BF16), jax.ShapeDtypeStruct((m, HEAD_DIM), BF16),
                   jax.ShapeDtypeStruct((m, HEAD_DIM), BF16), jax.ShapeDtypeStruct((m, 512), BF16),
                   jax.ShapeDtypeStruct((m, IDX_HEADS), F32)],
        compiler_params=_cparams(1),
        name="pre_even",
    )(h, ga, wg, wu, wd, gm, win, cos_t, sin_t)


def _pre_odd(h, ga, wg, wu, wd, gm, win, cos_t, sin_t, *, seq):
    m, d = h.shape
    tm = TOKEN_TILE
    nt = seq // tm
    row, rope = _token_specs(tm, d, nt)
    out = jax.ShapeDtypeStruct((m, d), F32)
    return pl.pallas_call(
        _pre_odd_kernel,
        grid=(m // tm,),
        in_specs=[row(d), _resident(ga.shape), _resident(wg.shape), _resident(wu.shape),
                  _resident(wd.shape), _resident(gm.shape), _resident(win.shape), rope, rope],
        out_specs=[row(d), row(d), row(d), row(d)],
        out_shape=[out, out, out, out],
        compiler_params=_cparams(1),
        name="pre_odd",
    )(h, ga, wg, wu, wd, gm, win, cos_t, sin_t)


def _post_kernel(*refs, n_mix, final):
    h_ref = refs[0]
    mix_refs = refs[1:1 + n_mix]
    (wout_ref, gb_ref, wg_ref, wu_ref, wd_ref, gp_ref, pg_ref, p_ref, pp_ref) = refs[1 + n_mix:10 + n_mix]
    rest = refs[10 + n_mix:]
    gf_ref, o_ref = (rest[0], rest[1]) if final else (None, rest[0])
    mix = [r[...] for r in mix_refs]
    mix = jnp.concatenate(mix, axis=1) if n_mix > 1 else mix[0]
    h1 = h_ref[...] + jnp.dot(mix, wout_ref[...], preferred_element_type=F32)
    h2 = _ffn_half(h1, gb_ref[...], wg_ref, wu_ref, wd_ref)
    z = jnp.dot(_rms(h2, gp_ref[...], NORM_EPS).astype(BF16), pg_ref[...], preferred_element_type=F32)
    gate = 1.0 / (1.0 + jnp.exp(-z))
    emb = jnp.dot(p_ref[...].astype(BF16), pp_ref[...], preferred_element_type=F32)
    h3 = h2 + gate * emb
    if final:
        h3 = _rms(h3, gf_ref[...], NORM_EPS)
    o_ref[...] = h3


def _post(h, mixes, wout, gb, wg, wu, wd, gp, pg, p, pp, gf):
    m, d = h.shape
    tm = TOKEN_TILE
    row = lambda w: pl.BlockSpec((tm, w), lambda i: (i, 0))
    final = gf is not None
    args = [h, *mixes, wout, gb, wg, wu, wd, gp, pg, p, pp]
    specs = [row(d), *[row(x.shape[1]) for x in mixes], _resident(wout.shape), _resident(gb.shape),
             _resident(wg.shape), _resident(wu.shape), _resident(wd.shape), _resident(gp.shape),
             _resident(pg.shape), row(p.shape[1]), _resident(pp.shape)]
    if final:
        args.append(gf)
        specs.append(_resident(gf.shape))
    return pl.pallas_call(
        functools.partial(_post_kernel, n_mix=len(mixes), final=final),
        grid=(m // tm,),
        in_specs=specs,
        out_specs=row(d),
        out_shape=jax.ShapeDtypeStruct((m, d), F32),
        compiler_params=_cparams(1),
        name="post",
    )(*args)


def _row_count(mask):
    return jnp.sum(jnp.where(mask, 1.0, 0.0), axis=1, keepdims=True)


def _topk_bias(score, causal, key_scr, idx_scr, *, top_k):
    tq, s = score.shape
    col = lax.broadcasted_iota(I32, (tq, s), 1)
    score = jnp.where(score == 0.0, 0.0, score)
    bits = pltpu.bitcast(score, I32)
    key = jnp.where(bits < 0, bits ^ jnp.int32(0x7FFFFFFF), bits)
    key_scr[...] = jnp.where(causal, key, jnp.int32(INT_MIN))

    def value_step(it, lo):
        cand = lo ^ lax.shift_left(jnp.int32(1), jnp.int32(31) - it)
        cnt = _row_count(key_scr[...] >= cand)
        return jnp.where(cnt >= top_k, cand, lo)

    lo = lax.fori_loop(0, 32, value_step, jnp.full((tq, 1), INT_MIN, I32))
    lo = jnp.maximum(lo, jnp.int32(INT_MIN + 1))
    key = key_scr[...]
    above = key > lo
    need = top_k - _row_count(above)
    idx_scr[...] = jnp.where(key == lo, col, jnp.int32(IDX_BIG))

    n_bits = max(1, int(math.ceil(math.log2(s))))

    def index_step(it, p):
        cand = p + lax.shift_left(jnp.int32(1), jnp.int32(n_bits - 1) - it)
        cnt = _row_count(idx_scr[...] < cand)
        return jnp.where(cnt < need, cand, p)

    p = lax.fori_loop(0, n_bits, index_step, jnp.zeros((tq, 1), I32))
    keep_tie = idx_scr[...] <= p
    return jnp.where(above, 0.0, jnp.where(keep_tie, 0.0, NEG_INF))


def _softmax_pv(s, v):
    m = jnp.max(s, axis=1, keepdims=True)
    p = jnp.exp(s - m)
    l = jnp.sum(p, axis=1, keepdims=True)
    return jnp.dot(p.astype(BF16), v, preferred_element_type=F32) * (1.0 / l)


def _dsa_kernel(qa_ref, qi_ref, wi_ref, ka_ref, va_ref, ki_ref, o_ref, *scratch, s_len, q_start, top_k):
    tq = o_ref.shape[0]
    ki = ki_ref[...]
    w = wi_ref[...]
    row = lax.broadcasted_iota(I32, (tq, s_len), 0) + q_start
    col = lax.broadcasted_iota(I32, (tq, s_len), 1)
    causal = col <= row
    if s_len > top_k:
        score = jnp.zeros((tq, s_len), F32)
        for hd in range(IDX_HEADS):
            dots = lax.dot_general(qi_ref[hd], ki, _CONTRACT_LAST, preferred_element_type=F32)
            score = score + w[:, hd:hd + 1] * jnp.maximum(dots, 0.0)
        bias = _topk_bias(score, causal, scratch[0], scratch[1], top_k=top_k)
    else:
        bias = jnp.where(causal, 0.0, NEG_INF)
    ka = ka_ref[...]
    va = va_ref[...]
    outs = []
    for hd in range(A_HEADS):
        s = lax.dot_general(qa_ref[hd], ka, _CONTRACT_LAST, preferred_element_type=F32) + bias
        outs.append(_softmax_pv(s, va))
    o_ref[...] = jnp.concatenate(outs, axis=1).astype(BF16)


def _dsa_bucket(qa, qi, wi, ka, va, ki, *, j, top_k):
    batch, _, seq, _ = qa.shape
    tq = ATTN_TQ
    s_len = (j + 1) * tq
    hm = pl.BlockSpec((None, 8, tq, HEAD_DIM), lambda b: (b, 0, j, 0))
    kv = pl.BlockSpec((None, s_len, HEAD_DIM), lambda b: (b, 0, 0))
    scratch = [pltpu.VMEM((tq, s_len), I32), pltpu.VMEM((tq, s_len), I32)] if s_len > top_k else []
    return pl.pallas_call(
        functools.partial(_dsa_kernel, s_len=s_len, q_start=j * tq, top_k=top_k),
        grid=(batch,),
        in_specs=[hm, hm, pl.BlockSpec((None, tq, IDX_HEADS), lambda b: (b, j, 0)), kv, kv, kv],
        out_specs=pl.BlockSpec((None, tq, A_HEADS * HEAD_DIM), lambda b: (b, 0, 0)),
        out_shape=jax.ShapeDtypeStruct((batch, tq, A_HEADS * HEAD_DIM), BF16),
        scratch_shapes=scratch,
        compiler_params=_cparams(1),
        name=f"dsa_{j}",
    )(qa, qi, wi, ka, va, ki)


def _diff_kernel(lam_ref, subln_ref, q_ref, k_ref, v_ref, o_ref, *, s_len, q_start, lambda_init):
    tq = o_ref.shape[0]
    lv = lam_ref[...]
    lam = (jnp.exp(jnp.sum(lv[0:1] * lv[1:2], axis=1, keepdims=True))
           - jnp.exp(jnp.sum(lv[2:3] * lv[3:4], axis=1, keepdims=True)) + lambda_init)
    row = lax.broadcasted_iota(I32, (tq, s_len), 0) + q_start
    col = lax.broadcasted_iota(I32, (tq, s_len), 1)
    bias = jnp.where(col <= row, 0.0, NEG_INF)
    v = v_ref[...]

    def component(c):
        s = lax.dot_general(q_ref[c], k_ref[c], _CONTRACT_LAST, preferred_element_type=F32) + bias
        return _softmax_pv(s, v)

    a = component(0) - lam * component(1)
    y = _rms(a, subln_ref[...], SUBLN_EPS) * (1.0 - lambda_init)
    o_ref[...] = y.astype(BF16)


def _diff_bucket(lam_vecs, subln, qb, kb, vb, *, j, lambda_init):
    batch, _, seq, _ = qb.shape
    tq = ATTN_TQ
    s_len = (j + 1) * tq
    return pl.pallas_call(
        functools.partial(_diff_kernel, s_len=s_len, q_start=j * tq, lambda_init=lambda_init),
        grid=(batch, B_HEADS),
        in_specs=[pl.BlockSpec(lam_vecs.shape, lambda b, h: (0, 0)),
                  pl.BlockSpec(subln.shape, lambda b, h: (0, 0)),
                  pl.BlockSpec((None, 2, tq, HEAD_DIM), lambda b, h: (b, h, j, 0)),
                  pl.BlockSpec((None, 2, s_len, HEAD_DIM), lambda b, h: (b, h, 0, 0)),
                  pl.BlockSpec((None, s_len, B_VDIM), lambda b, h: (b, 0, h))],
        out_specs=pl.BlockSpec((None, tq, B_VDIM), lambda b, h: (b, 0, h)),
        out_shape=jax.ShapeDtypeStruct((batch, tq, B_HEADS * B_VDIM), BF16),
        compiler_params=_cparams(2),
        name=f"diff_{j}",
    )(lam_vecs, subln, qb, kb, vb)


def _dil_kernel(q_ref, k_ref, v_ref, o_ref, m_scr, l_scr, acc_scr):
    seq = q_ref.shape[0]
    span = DIL_SPAN
    lane = lax.broadcasted_iota(I32, (span, LANES), 1)
    head0 = lane < HEAD_DIM

    def rows_at(start, d):
        return pl.ds(start, span) if d == 1 else pl.ds(start, span, stride=d)

    def tile(t, *, d, blk0, first):
        r = t % d
        blk = blk0 + t // d
        start = r + blk * (span * d)
        if d == 1:
            start = pl.multiple_of(start, span)
        rows = rows_at(start, d)
        qb = q_ref[rows, :]
        qs = jnp.concatenate([jnp.where(head0, qb, 0.0), jnp.where(head0, 0.0, qb)], axis=0).astype(BF16)
        kk = k_ref[rows, :]
        vv = v_ref[rows, :]
        has_prev = blk0 > 0
        if has_prev:
            prev = rows_at(start - span * d, d)
            kk = jnp.concatenate([k_ref[prev, :], kk], axis=0)
            vv = jnp.concatenate([v_ref[prev, :], vv], axis=0)
        nk = kk.shape[0]
        s = lax.dot_general(qs, kk.astype(BF16), _CONTRACT_LAST, preferred_element_type=F32)
        qi = lax.broadcasted_iota(I32, (2 * span, nk), 0) & (span - 1)
        kj = lax.broadcasted_iota(I32, (2 * span, nk), 1)
        if has_prev:
            dist = kj - qi
            ok = jnp.where(kj < span, dist, span - dist) >= 0
        else:
            ok = kj <= qi
        s = jnp.where(ok, s, NEG_INF)
        m = jnp.max(s, axis=1, keepdims=True)
        p = jnp.exp(s - m)
        l = jnp.sum(p, axis=1, keepdims=True)
        pv = jnp.dot(p.astype(BF16), vv.astype(BF16), preferred_element_type=F32)
        acc_t = jnp.where(head0, pv[:span], pv[span:])
        m_t = jnp.where(head0, m[:span], m[span:])
        l_t = jnp.where(head0, l[:span], l[span:])
        if first:
            m_scr[rows, :] = m_t
            l_scr[rows, :] = l_t
            acc_scr[rows, :] = acc_t
        else:
            m_o = m_scr[rows, :]
            m_n = jnp.maximum(m_o, m_t)
            ea = jnp.exp(m_o - m_n)
            eb = jnp.exp(m_t - m_n)
            l_scr[rows, :] = ea * l_scr[rows, :] + eb * l_t
            acc_scr[rows, :] = ea * acc_scr[rows, :] + eb * acc_t
            m_scr[rows, :] = m_n

    for pi, (window, d) in enumerate(C_PATTERNS):
        n_blocks = (seq // d) // span
        first = pi == 0

        def no_prev(t, carry, d=d, first=first):
            tile(t, d=d, blk0=0, first=first)
            return carry

        def with_prev(t, carry, d=d, first=first):
            tile(t, d=d, blk0=1, first=first)
            return carry

        lax.fori_loop(0, d, no_prev, 0)
        if n_blocks > 1:
            lax.fori_loop(0, d * (n_blocks - 1), with_prev, 0)

    o_ref[...] = (acc_scr[...] * (1.0 / l_scr[...])).astype(BF16)


def _dilated(q, k, v):
    batch, seq, d = q.shape
    n_pairs = d // LANES
    blk = lambda: pl.BlockSpec((None, seq, LANES), lambda b, hp: (b, 0, hp))
    return pl.pallas_call(
        _dil_kernel,
        grid=(batch, n_pairs),
        in_specs=[blk(), blk(), blk()],
        out_specs=blk(),
        out_shape=jax.ShapeDtypeStruct((batch, seq, d), BF16),
        scratch_shapes=[pltpu.VMEM((seq, LANES), F32)] * 3,
        compiler_params=_cparams(2),
        name="dilated",
    )(q, k, v)


def _rope_tables(seq):
    inv = 1.0 / (ROPE_THETA ** (jnp.arange(0, HEAD_DIM, 2, dtype=F32) / HEAD_DIM))
    ang = jnp.arange(seq, dtype=F32)[:, None] * inv[None, :]
    cos, sin = jnp.cos(ang), jnp.sin(ang)
    cos_t = jnp.concatenate([cos, cos, cos, cos], axis=1)
    sin_t = jnp.concatenate([-sin, sin, -sin, sin], axis=1)
    return cos_t, sin_t


def _pack_even_w_in(w):
    cuts = np.cumsum([0, 512, 64, 64, 512, 64, 8, 512, 512, 512])
    qa, ka, va, qi, ki, wi, qb, kb, vb = [w[:, cuts[i]:cuts[i + 1]] for i in range(9)]
    packed = jnp.concatenate([qa, qi, qb, kb, ka, ki, vb, va, wi], axis=1)
    pad = (-packed.shape[1]) % LANES
    return jnp.pad(packed, ((0, 0), (0, pad))).astype(BF16)


def kernel(x, p, norm_ffn_a, ffn_a_wg, ffn_a_wu, ffn_a_wd, norm_mix, norm_ffn_b, ffn_b_wg, ffn_b_wu,
           ffn_b_wd, norm_ple, ple_gate, ple_proj, even_w_in, even_w_out, diff_lambda_q1,
           diff_lambda_k1, diff_lambda_q2, diff_lambda_k2, diff_subln, odd_w_in, odd_w_out, final_norm):
    batch, seq, d_model = x.shape
    depth = p.shape[0]
    assert seq % ATTN_TQ == 0 and (batch * seq) % TOKEN_TILE == 0 and seq % TOKEN_TILE == 0
    assert seq == C_PATTERNS[-1][0], "dilated kernel assumes the widest window spans the sequence"
    top_k = min(A_TOPK_MAX, seq // 4)
    cos_t, sin_t = _rope_tables(seq)
    gain = lambda g: g.reshape(1, -1).astype(F32)
    bf = lambda w: w.astype(BF16)
    n_buckets = seq // ATTN_TQ

    h = x.reshape(batch * seq, d_model)
    for i in range(depth):
        pre_args = (h, gain(norm_ffn_a[i]), bf(ffn_a_wg[i]), bf(ffn_a_wu[i]), bf(ffn_a_wd[i]),
                    gain(norm_mix[i]))
        if i % 2 == 0:
            e = i // 2
            lambda_init = 0.8 - 0.6 * math.exp(-0.3 * i)
            (ha, qa, qi, qb, kb, ka, ki, va, vb, wi) = _pre_even(
                *pre_args, _pack_even_w_in(even_w_in[e]), cos_t, sin_t, batch=batch, seq=seq)
            r3 = lambda a: a.reshape(batch, seq, a.shape[-1])
            ka, ki, va, vb, wi = r3(ka), r3(ki), r3(va), r3(vb), r3(wi)
            lam_vecs = jnp.stack([diff_lambda_q1[e], diff_lambda_k1[e],
                                  diff_lambda_q2[e], diff_lambda_k2[e]]).astype(F32)
            subln = gain(diff_subln[e])
            out_a = [_dsa_bucket(qa, qi, wi, ka, va, ki, j=j, top_k=top_k) for j in range(n_buckets)]
            out_b = [_diff_bucket(lam_vecs, subln, qb, kb, vb, j=j, lambda_init=lambda_init)
                     for j in range(n_buckets)]
            mixes = [jnp.concatenate(out_a, axis=1).reshape(batch * seq, -1),
                     jnp.concatenate(out_b, axis=1).reshape(batch * seq, -1)]
            w_out = bf(even_w_out[e])
        else:
            o = i // 2
            ha, q, k, v = _pre_odd(*pre_args, bf(odd_w_in[o]), cos_t, sin_t, seq=seq)
            r3 = lambda a: a.reshape(batch, seq, a.shape[-1])
            mixes = [_dilated(r3(q), r3(k), r3(v)).reshape(batch * seq, -1)]
            w_out = bf(odd_w_out[o])
        gf = gain(final_norm) if i == depth - 1 else None
        h = _post(ha, mixes, w_out, gain(norm_ffn_b[i]), bf(ffn_b_wg[i]), bf(ffn_b_wu[i]),
                  bf(ffn_b_wd[i]), gain(norm_ple[i]), bf(ple_gate[i]),
                  p[i].reshape(batch * seq, -1), bf(ple_proj[i]), gf)
    return h.reshape(batch, seq, d_model)
```

```python
import functools
import math

import jax
import jax.numpy as jnp
import numpy as np
from jax import lax
from jax.experimental import pallas as pl
from jax.experimental.pallas import tpu as pltpu

F32 = jnp.float32
BF16 = jnp.bfloat16
I32 = jnp.int32

HEAD_DIM = 64
ROPE_THETA = 10000.0
NORM_EPS = 1e-6
SUBLN_EPS = 1e-5
NEG_INF = -1e30
A_HEADS = 8
A_TOPK_MAX = 256
IDX_HEADS = 8
B_HEADS = 4
B_VDIM = 2 * HEAD_DIM
C_PATTERNS = ((128, 1), (512, 4), (2048, 16))
DIL_SPAN = 128

LANES = 128
VMEM_LIMIT_BYTES = 56 * 1024 * 1024
TOKEN_TILE = 256
FFN_CHUNKS = 2
ATTN_TQ = 256
DIL_MAX_UNROLL = 5

INT_MIN = -(2 ** 31)
KEY_NEG_INF = INT_MIN + 0x7FFFFF
KEY_POS_INF = 0x7F800000
IDX_BIG = 2 ** 30

_CONTRACT_LAST = (((1,), (1,)), ((), ()))


def _cparams(n_grid):
    return pltpu.CompilerParams(dimension_semantics=("arbitrary",) * n_grid,
                                vmem_limit_bytes=VMEM_LIMIT_BYTES)


def _resident(shape):
    nd = len(shape)
    return pl.BlockSpec(shape, lambda *_: (0,) * nd, pipeline_mode=pl.Buffered(1))


def _rms(x, g, eps):
    ms = jnp.mean(x * x, axis=-1, keepdims=True)
    return x * lax.rsqrt(ms + eps) * g


def _ffn_half(h, g, wg_ref, wu_ref, wd_ref):
    xn = _rms(h, g, NORM_EPS).astype(BF16)
    d_ff = wg_ref.shape[1]
    fc = d_ff // FFN_CHUNKS
    acc = None
    for c in range(FFN_CHUNKS):
        a = jnp.dot(xn, wg_ref[:, c * fc:(c + 1) * fc], preferred_element_type=F32)
        b = jnp.dot(xn, wu_ref[:, c * fc:(c + 1) * fc], preferred_element_type=F32)
        mid = (a / (1.0 + jnp.exp(-a)) * b).astype(BF16)
        part = jnp.dot(mid, wd_ref[c * fc:(c + 1) * fc, :], preferred_element_type=F32)
        acc = part if acc is None else acc + part
    return h + 0.5 * acc


def _rope(y, cos_t, sin_t):
    w = y.shape[1]
    n = w // LANES
    c = jnp.concatenate([cos_t] * n, axis=1) if n > 1 else cos_t
    s = jnp.concatenate([sin_t] * n, axis=1) if n > 1 else sin_t
    lane = lax.broadcasted_iota(I32, y.shape, 1)
    first_half = (lane & (HEAD_DIM // 2)) == 0
    partner = jnp.where(first_half,
                        pltpu.roll(y, w - HEAD_DIM // 2, 1),
                        pltpu.roll(y, HEAD_DIM // 2, 1))
    return y * c + partner * s


def _pre_even_kernel(h_ref, ga_ref, wg_ref, wu_ref, wd_ref, gm_ref, win_ref, cos_ref, sin_ref,
                     ha_ref, qa_ref, qi_ref, qb_ref, kb_ref, ka_ref, ki_ref, va_ref, vb_ref, wi_ref,
                     *, idx_scale):
    ha = _ffn_half(h_ref[...], ga_ref[...], wg_ref, wu_ref, wd_ref)
    ha_ref[...] = ha
    xn = _rms(ha, gm_ref[...], NORM_EPS).astype(BF16)
    y = jnp.dot(xn, win_ref[...], preferred_element_type=F32)
    n_rope = 4 * 512 + 128
    yr = _rope(y[:, :n_rope], cos_ref[...], sin_ref[...])
    q_scale = HEAD_DIM ** -0.5
    for hd in range(8):
        lo = hd * HEAD_DIM
        qa_ref[hd] = (yr[:, lo:lo + HEAD_DIM] * q_scale).astype(BF16)
        qi_ref[hd] = yr[:, 512 + lo:512 + lo + HEAD_DIM].astype(BF16)
        qb_ref[hd] = (yr[:, 1024 + lo:1024 + lo + HEAD_DIM] * q_scale).astype(BF16)
        kb_ref[hd] = yr[:, 1536 + lo:1536 + lo + HEAD_DIM].astype(BF16)
    ka_ref[...] = yr[:, 2048:2112].astype(BF16)
    ki_ref[...] = yr[:, 2112:2176].astype(BF16)
    vb_ref[...] = y[:, 2176:2688].astype(BF16)
    va_ref[...] = y[:, 2688:2752].astype(BF16)
    wi_ref[...] = y[:, 2752:2760] * idx_scale


def _pre_odd_kernel(h_ref, ga_ref, wg_ref, wu_ref, wd_ref, gm_ref, win_ref, cos_ref, sin_ref,
                    ha_ref, q_ref, k_ref, v_ref):
    ha = _ffn_half(h_ref[...], ga_ref[...], wg_ref, wu_ref, wd_ref)
    ha_ref[...] = ha
    xn = _rms(ha, gm_ref[...], NORM_EPS).astype(BF16)
    y = jnp.dot(xn, win_ref[...], preferred_element_type=F32)
    d = q_ref.shape[1]
    yr = _rope(y[:, :2 * d], cos_ref[...], sin_ref[...])
    q_ref[...] = yr[:, :d] * (HEAD_DIM ** -0.5)
    k_ref[...] = yr[:, d:2 * d]
    v_ref[...] = y[:, 2 * d:3 * d]


def _token_specs(tm, d, seq_tiles):
    row = lambda w: pl.BlockSpec((tm, w), lambda i: (i, 0))
    rope = pl.BlockSpec((tm, LANES), lambda i: (i % seq_tiles, 0))
    return row, rope


def _pre_even(h, ga, wg, wu, wd, gm, win, cos_t, sin_t, *, batch, seq):
    m, d = h.shape
    tm = TOKEN_TILE
    nt = seq // tm
    row, rope = _token_specs(tm, d, nt)
    hm = lambda: pl.BlockSpec((None, 8, tm, HEAD_DIM), lambda i: (i // nt, 0, i % nt, 0))
    hm_shape = jax.ShapeDtypeStruct((batch, 8, seq, HEAD_DIM), BF16)
    idx_scale = (HEAD_DIM ** -0.5) * (IDX_HEADS ** -0.5)
    return pl.pallas_call(
        functools.partial(_pre_even_kernel, idx_scale=idx_scale),
        grid=(m // tm,),
        in_specs=[row(d), _resident(ga.shape), _resident(wg.shape), _resident(wu.shape),
                  _resident(wd.shape), _resident(gm.shape), _resident(win.shape), rope, rope],
        out_specs=[row(d), hm(), hm(), hm(), hm(), row(HEAD_DIM), row(HEAD_DIM), row(HEAD_DIM),
                   row(512), row(IDX_HEADS)],
        out_shape=[jax.ShapeDtypeStruct((m, d), F32), hm_shape, hm_shape, hm_shape, hm_shape,
                   jax.ShapeDtypeStruct((m, HEAD_DIM), BF16), jax.ShapeDtypeStruct((m, HEAD_DIM), BF16),
                   jax.ShapeDtypeStruct((m, HEAD_DIM), BF16), jax.ShapeDtypeStruct((m, 512), BF16),
                   jax.ShapeDtypeStruct((m, IDX_HEADS), F32)],
        compiler_params=_cparams(1),
        name="pre_even",
    )(h, ga, wg, wu, wd, gm, win, cos_t, sin_t)


def _pre_odd(h, ga, wg, wu, wd, gm, win, cos_t, sin_t, *, seq):
    m, d = h.shape
    tm = TOKEN_TILE
    nt = seq // tm
    row, rope = _token_specs(tm, d, nt)
    out = jax.ShapeDtypeStruct((m, d), F32)
    return pl.pallas_call(
        _pre_odd_kernel,
        grid=(m // tm,),
        in_specs=[row(d), _resident(ga.shape), _resident(wg.shape), _resident(wu.shape),
                  _resident(wd.shape), _resident(gm.shape), _resident(win.shape), rope, rope],
        out_specs=[row(d), row(d), row(d), row(d)],
        out_shape=[out, out, out, out],
        compiler_params=_cparams(1),
        name="pre_odd",
    )(h, ga, wg, wu, wd, gm, win, cos_t, sin_t)


def _post_kernel(*refs, n_mix, final):
    h_ref = refs[0]
    mix_refs = refs[1:1 + n_mix]
    (wout_ref, gb_ref, wg_ref, wu_ref, wd_ref, gp_ref, pg_ref, p_ref, pp_ref) = refs[1 + n_mix:10 + n_mix]
    rest = refs[10 + n_mix:]
    gf_ref, o_ref = (rest[0], rest[1]) if final else (None, rest[0])
    mix = [r[...] for r in mix_refs]
    mix = jnp.concatenate(mix, axis=1) if n_mix > 1 else mix[0]
    h1 = h_ref[...] + jnp.dot(mix, wout_ref[...], preferred_element_type=F32)
    h2 = _ffn_half(h1, gb_ref[...], wg_ref, wu_ref, wd_ref)
    z = jnp.dot(_rms(h2, gp_ref[...], NORM_EPS).astype(BF16), pg_ref[...], preferred_element_type=F32)
    gate = 1.0 / (1.0 + jnp.exp(-z))
    emb = jnp.dot(p_ref[...].astype(BF16), pp_ref[...], preferred_element_type=F32)
    h3 = h2 + gate * emb
    if final:
        h3 = _rms(h3, gf_ref[...], NORM_EPS)
    o_ref[...] = h3


def _post(h, mixes, wout, gb, wg, wu, wd, gp, pg, p, pp, gf):
    m, d = h.shape
    tm = TOKEN_TILE
    row = lambda w: pl.BlockSpec((tm, w), lambda i: (i, 0))
    final = gf is not None
    args = [h, *mixes, wout, gb, wg, wu, wd, gp, pg, p, pp]
    specs = [row(d), *[row(x.shape[1]) for x in mixes], _resident(wout.shape), _resident(gb.shape),
             _resident(wg.shape), _resident(wu.shape), _resident(wd.shape), _resident(gp.shape),
             _resident(pg.shape), row(p.shape[1]), _resident(pp.shape)]
    if final:
        args.append(gf)
        specs.append(_resident(gf.shape))
    return pl.pallas_call(
        functools.partial(_post_kernel, n_mix=len(mixes), final=final),
        grid=(m // tm,),
        in_specs=specs,
        out_specs=row(d),
        out_shape=jax.ShapeDtypeStruct((m, d), F32),
        compiler_params=_cparams(1),
        name="post",
    )(*args)


def _row_count(mask):
    return jnp.sum(jnp.where(mask, 1.0, 0.0), axis=1, keepdims=True)


def _key_to_float(key):
    key = jnp.clip(key, jnp.int32(KEY_NEG_INF), jnp.int32(KEY_POS_INF))
    return pltpu.bitcast(jnp.where(key < 0, key ^ jnp.int32(0x7FFFFFFF), key), F32)


def _topk_bias(score, causal, val_scr, idx_scr, *, top_k):
    tq, s = score.shape
    col = lax.broadcasted_iota(I32, (tq, s), 1)
    val_scr[...] = jnp.where(causal, score, -jnp.inf)

    def value_step(it, lo):
        cand = lo ^ lax.shift_left(jnp.int32(1), jnp.int32(31) - it)
        cnt = _row_count(val_scr[...] >= _key_to_float(cand))
        return jnp.where(cnt >= top_k, cand, lo)

    lo = lax.fori_loop(0, 32, value_step, jnp.full((tq, 1), INT_MIN, I32))
    thr = _key_to_float(lo)
    vals = val_scr[...]
    above = vals > thr
    need = top_k - _row_count(above)
    idx_scr[...] = jnp.where(vals == thr, col, jnp.int32(IDX_BIG))

    n_bits = max(1, int(math.ceil(math.log2(s))))

    def index_step(it, p):
        cand = p + lax.shift_left(jnp.int32(1), jnp.int32(n_bits - 1) - it)
        cnt = _row_count(idx_scr[...] < cand)
        return jnp.where(cnt < need, cand, p)

    p = lax.fori_loop(0, n_bits, index_step, jnp.zeros((tq, 1), I32))
    keep_tie = idx_scr[...] <= p
    keep = jnp.where(above, 0.0, jnp.where(keep_tie, 0.0, NEG_INF))
    return jnp.where(causal, keep, NEG_INF)


def _softmax_pv(s, v):
    m = jnp.max(s, axis=1, keepdims=True)
    p = jnp.exp(s - m)
    l = jnp.sum(p, axis=1, keepdims=True)
    return jnp.dot(p.astype(BF16), v, preferred_element_type=F32) * (1.0 / l)


def _dsa_kernel(qa_ref, qi_ref, wi_ref, ka_ref, va_ref, ki_ref, o_ref, *scratch, s_len, q_start, top_k):
    tq = o_ref.shape[0]
    ki = ki_ref[...]
    w = wi_ref[...]
    row = lax.broadcasted_iota(I32, (tq, s_len), 0) + q_start
    col = lax.broadcasted_iota(I32, (tq, s_len), 1)
    causal = col <= row
    if s_len > top_k:
        score = jnp.zeros((tq, s_len), F32)
        for hd in range(IDX_HEADS):
            dots = lax.dot_general(qi_ref[hd], ki, _CONTRACT_LAST, preferred_element_type=F32)
            score = score + w[:, hd:hd + 1] * jnp.maximum(dots, 0.0)
        bias = _topk_bias(score, causal, scratch[0], scratch[1], top_k=top_k)
    else:
        bias = jnp.where(causal, 0.0, NEG_INF)
    ka = ka_ref[...]
    va = va_ref[...]
    outs = []
    for hd in range(A_HEADS):
        s = lax.dot_general(qa_ref[hd], ka, _CONTRACT_LAST, preferred_element_type=F32) + bias
        outs.append(_softmax_pv(s, va))
    o_ref[...] = jnp.concatenate(outs, axis=1).astype(BF16)


def _dsa_bucket(qa, qi, wi, ka, va, ki, *, j, top_k):
    batch, _, seq, _ = qa.shape
    tq = ATTN_TQ
    s_len = (j + 1) * tq
    hm = pl.BlockSpec((None, 8, tq, HEAD_DIM), lambda b: (b, 0, j, 0))
    kv = pl.BlockSpec((None, s_len, HEAD_DIM), lambda b: (b, 0, 0))
    scratch = [pltpu.VMEM((tq, s_len), F32), pltpu.VMEM((tq, s_len), I32)] if s_len > top_k else []
    return pl.pallas_call(
        functools.partial(_dsa_kernel, s_len=s_len, q_start=j * tq, top_k=top_k),
        grid=(batch,),
        in_specs=[hm, hm, pl.BlockSpec((None, tq, IDX_HEADS), lambda b: (b, j, 0)), kv, kv, kv],
        out_specs=pl.BlockSpec((None, tq, A_HEADS * HEAD_DIM), lambda b: (b, 0, 0)),
        out_shape=jax.ShapeDtypeStruct((batch, tq, A_HEADS * HEAD_DIM), BF16),
        scratch_shapes=scratch,
        compiler_params=_cparams(1),
        name=f"dsa_{j}",
    )(qa, qi, wi, ka, va, ki)


def _diff_kernel(lam_ref, subln_ref, q_ref, k_ref, v_ref, o_ref, *, s_len, q_start, lambda_init):
    tq = o_ref.shape[0]
    lv = lam_ref[...]
    lam = (jnp.exp(jnp.sum(lv[0:1] * lv[1:2], axis=1, keepdims=True))
           - jnp.exp(jnp.sum(lv[2:3] * lv[3:4], axis=1, keepdims=True)) + lambda_init)
    row = lax.broadcasted_iota(I32, (tq, s_len), 0) + q_start
    col = lax.broadcasted_iota(I32, (tq, s_len), 1)
    bias = jnp.where(col <= row, 0.0, NEG_INF)
    v = v_ref[...]

    def component(c):
        s = lax.dot_general(q_ref[c], k_ref[c], _CONTRACT_LAST, preferred_element_type=F32) + bias
        return _softmax_pv(s, v)

    a = component(0) - lam * component(1)
    y = _rms(a, subln_ref[...], SUBLN_EPS) * (1.0 - lambda_init)
    o_ref[...] = y.astype(BF16)


def _diff_bucket(lam_vecs, subln, qb, kb, vb, *, j, lambda_init):
    batch, _, seq, _ = qb.shape
    tq = ATTN_TQ
    s_len = (j + 1) * tq
    return pl.pallas_call(
        functools.partial(_diff_kernel, s_len=s_len, q_start=j * tq, lambda_init=lambda_init),
        grid=(batch, B_HEADS),
        in_specs=[pl.BlockSpec(lam_vecs.shape, lambda b, h: (0, 0)),
                  pl.BlockSpec(subln.shape, lambda b, h: (0, 0)),
                  pl.BlockSpec((None, 2, tq, HEAD_DIM), lambda b, h: (b, h, j, 0)),
                  pl.BlockSpec((None, 2, s_len, HEAD_DIM), lambda b, h: (b, h, 0, 0)),
                  pl.BlockSpec((None, s_len, B_VDIM), lambda b, h: (b, 0, h))],
        out_specs=pl.BlockSpec((None, tq, B_VDIM), lambda b, h: (b, 0, h)),
        out_shape=jax.ShapeDtypeStruct((batch, tq, B_HEADS * B_VDIM), BF16),
        compiler_params=_cparams(2),
        name=f"diff_{j}",
    )(lam_vecs, subln, qb, kb, vb)


def _unroll_for(trips):
    return max(u for u in range(1, DIL_MAX_UNROLL + 1) if trips % u == 0)


def _dil_kernel(q_ref, k_ref, v_ref, o_ref, m_scr, l_scr, acc_scr):
    seq = q_ref.shape[0]
    span = DIL_SPAN
    lane = lax.broadcasted_iota(I32, (span, LANES), 1)
    head0 = lane < HEAD_DIM

    def rows_at(start, d):
        return pl.ds(start, span) if d == 1 else pl.ds(start, span, stride=d)

    def tile(t, *, d, blk0, first):
        r = t % d
        blk = blk0 + t // d
        start = r + blk * (span * d)
        if d == 1:
            start = pl.multiple_of(start, span)
        rows = rows_at(start, d)
        qb = q_ref[rows, :]
        qs = jnp.concatenate([jnp.where(head0, qb, 0.0), jnp.where(head0, 0.0, qb)], axis=0).astype(BF16)
        kk = k_ref[rows, :]
        vv = v_ref[rows, :]
        has_prev = blk0 > 0
        if has_prev:
            prev = rows_at(start - span * d, d)
            kk = jnp.concatenate([k_ref[prev, :], kk], axis=0)
            vv = jnp.concatenate([v_ref[prev, :], vv], axis=0)
        nk = kk.shape[0]
        s = lax.dot_general(qs, kk.astype(BF16), _CONTRACT_LAST, preferred_element_type=F32)
        qi = lax.broadcasted_iota(I32, (2 * span, nk), 0) & (span - 1)
        kj = lax.broadcasted_iota(I32, (2 * span, nk), 1)
        if has_prev:
            dist = kj - qi
            ok = jnp.where(kj < span, dist, span - dist) >= 0
        else:
            ok = kj <= qi
        s = jnp.where(ok, s, NEG_INF)
        m = jnp.max(s, axis=1, keepdims=True)
        p = jnp.exp(s - m)
        l = jnp.sum(p, axis=1, keepdims=True)
        pv = jnp.dot(p.astype(BF16), vv.astype(BF16), preferred_element_type=F32)
        acc_t = jnp.where(head0, pv[:span], pv[span:])
        m_t = jnp.where(head0, m[:span], m[span:])
        l_t = jnp.where(head0, l[:span], l[span:])
        if first:
            m_scr[rows, :] = m_t
            l_scr[rows, :] = l_t
            acc_scr[rows, :] = acc_t
        else:
            m_o = m_scr[rows, :]
            m_n = jnp.maximum(m_o, m_t)
            ea = jnp.exp(m_o - m_n)
            eb = jnp.exp(m_t - m_n)
            l_scr[rows, :] = ea * l_scr[rows, :] + eb * l_t
            acc_scr[rows, :] = ea * acc_scr[rows, :] + eb * acc_t
            m_scr[rows, :] = m_n

    for pi, (window, d) in enumerate(C_PATTERNS):
        n_blocks = (seq // d) // span
        first = pi == 0

        def no_prev(t, carry, d=d, first=first):
            tile(t, d=d, blk0=0, first=first)
            return carry

        def with_prev(t, carry, d=d, first=first):
            tile(t, d=d, blk0=1, first=first)
            return carry

        lax.fori_loop(0, d, no_prev, 0, unroll=_unroll_for(d))
        if n_blocks > 1:
            trips = d * (n_blocks - 1)
            lax.fori_loop(0, trips, with_prev, 0, unroll=_unroll_for(trips))

    o_ref[...] = (acc_scr[...] * (1.0 / l_scr[...])).astype(BF16)


def _dilated(q, k, v):
    batch, seq, d = q.shape
    n_pairs = d // LANES
    blk = lambda: pl.BlockSpec((None, seq, LANES), lambda b, hp: (b, 0, hp))
    return pl.pallas_call(
        _dil_kernel,
        grid=(batch, n_pairs),
        in_specs=[blk(), blk(), blk()],
        out_specs=blk(),
        out_shape=jax.ShapeDtypeStruct((batch, seq, d), BF16),
        scratch_shapes=[pltpu.VMEM((seq, LANES), F32)] * 3,
        compiler_params=_cparams(2),
        name="dilated",
    )(q, k, v)


def _rope_tables(seq):
    inv = 1.0 / (ROPE_THETA ** (jnp.arange(0, HEAD_DIM, 2, dtype=F32) / HEAD_DIM))
    ang = jnp.arange(seq, dtype=F32)[:, None] * inv[None, :]
    cos, sin = jnp.cos(ang), jnp.sin(ang)
    cos_t = jnp.concatenate([cos, cos, cos, cos], axis=1)
    sin_t = jnp.concatenate([-sin, sin, -sin, sin], axis=1)
    return cos_t, sin_t


def _pack_even_w_in(w):
    cuts = np.cumsum([0, 512, 64, 64, 512, 64, 8, 512, 512, 512])
    qa, ka, va, qi, ki, wi, qb, kb, vb = [w[:, cuts[i]:cuts[i + 1]] for i in range(9)]
    packed = jnp.concatenate([qa, qi, qb, kb, ka, ki, vb, va, wi], axis=1)
    pad = (-packed.shape[1]) % LANES
    return jnp.pad(packed, ((0, 0), (0, pad))).astype(BF16)


def kernel(x, p, norm_ffn_a, ffn_a_wg, ffn_a_wu, ffn_a_wd, norm_mix, norm_ffn_b, ffn_b_wg, ffn_b_wu,
           ffn_b_wd, norm_ple, ple_gate, ple_proj, even_w_in, even_w_out, diff_lambda_q1,
           diff_lambda_k1, diff_lambda_q2, diff_lambda_k2, diff_subln, odd_w_in, odd_w_out, final_norm):
    batch, seq, d_model = x.shape
    depth = p.shape[0]
    assert seq % ATTN_TQ == 0 and (batch * seq) % TOKEN_TILE == 0 and seq % TOKEN_TILE == 0
    assert seq == C_PATTERNS[-1][0], "dilated kernel assumes the widest window spans the sequence"
    top_k = min(A_TOPK_MAX, seq // 4)
    cos_t, sin_t = _rope_tables(seq)
    gain = lambda g: g.reshape(1, -1).astype(F32)
    bf = lambda w: w.astype(BF16)
    n_buckets = seq // ATTN_TQ

    h = x.reshape(batch * seq, d_model)
    for i in range(depth):
        pre_args = (h, gain(norm_ffn_a[i]), bf(ffn_a_wg[i]), bf(ffn_a_wu[i]), bf(ffn_a_wd[i]),
                    gain(norm_mix[i]))
        if i % 2 == 0:
            e = i // 2
            lambda_init = 0.8 - 0.6 * math.exp(-0.3 * i)
            (ha, qa, qi, qb, kb, ka, ki, va, vb, wi) = _pre_even(
                *pre_args, _pack_even_w_in(even_w_in[e]), cos_t, sin_t, batch=batch, seq=seq)
            r3 = lambda a: a.reshape(batch, seq, a.shape[-1])
            ka, ki, va, vb, wi = r3(ka), r3(ki), r3(va), r3(vb), r3(wi)
            lam_vecs = jnp.stack([diff_lambda_q1[e], diff_lambda_k1[e],
                                  diff_lambda_q2[e], diff_lambda_k2[e]]).astype(F32)
            subln = gain(diff_subln[e])
            out_a = [_dsa_bucket(qa, qi, wi, ka, va, ki, j=j, top_k=top_k) for j in range(n_buckets)]
            out_b = [_diff_bucket(lam_vecs, subln, qb, kb, vb, j=j, lambda_init=lambda_init)
                     for j in range(n_buckets)]
            mixes = [jnp.concatenate(out_a, axis=1).reshape(batch * seq, -1),
                     jnp.concatenate(out_b, axis=1).reshape(batch * seq, -1)]
            w_out = bf(even_w_out[e])
        else:
            o = i // 2
            ha, q, k, v = _pre_odd(*pre_args, bf(odd_w_in[o]), cos_t, sin_t, seq=seq)
            r3 = lambda a: a.reshape(batch, seq, a.shape[-1])
            mixes = [_dilated(r3(q), r3(k), r3(v)).reshape(batch * seq, -1)]
            w_out = bf(odd_w_out[o])
        gf = gain(final_norm) if i == depth - 1 else None
        h = _post(ha, mixes, w_out, gain(norm_ffn_b[i]), bf(ffn_b_wg[i]), bf(ffn_b_wu[i]),
                  bf(ffn_b_wd[i]), gain(norm_ple[i]), bf(ple_gate[i]),
                  p[i].reshape(batch * seq, -1), bf(ple_proj[i]), gf)
    return h.reshape(batch, seq, d_model)
```

```python
import functools
import math

import jax
import jax.numpy as jnp
import numpy as np
from jax import lax
from jax.experimental import pallas as pl
from jax.experimental.pallas import tpu as pltpu

F32 = jnp.float32
BF16 = jnp.bfloat16
I32 = jnp.int32

HEAD_DIM = 64
ROPE_THETA = 10000.0
NORM_EPS = 1e-6
SUBLN_EPS = 1e-5
NEG_INF = -1e30
A_HEADS = 8
A_TOPK_MAX = 256
IDX_HEADS = 8
B_HEADS = 4
B_VDIM = 2 * HEAD_DIM
C_PATTERNS = ((128, 1), (512, 4), (2048, 16))
DIL_SPAN = 128

LANES = 128
VMEM_LIMIT_BYTES = 56 * 1024 * 1024
TOKEN_TILE = 256
FFN_CHUNKS = 2
ATTN_TQ = 256
DIL_MAX_UNROLL = 5

ATTN_Q_SCALE = (HEAD_DIM ** -0.5) * math.log2(math.e)

INT_MIN = -(2 ** 31)
KEY_NEG_INF = INT_MIN + 0x7FFFFF
KEY_POS_INF = 0x7F800000
IDX_BIG = 2 ** 30

_CONTRACT_LAST = (((1,), (1,)), ((), ()))


def _cparams(n_grid):
    return pltpu.CompilerParams(dimension_semantics=("arbitrary",) * n_grid,
                                vmem_limit_bytes=VMEM_LIMIT_BYTES)


def _resident(shape):
    nd = len(shape)
    return pl.BlockSpec(shape, lambda *_: (0,) * nd, pipeline_mode=pl.Buffered(1))


def _rms(x, g, eps):
    ms = jnp.mean(x * x, axis=-1, keepdims=True)
    return x * lax.rsqrt(ms + eps) * g


def _ffn_half(h, g, wg_ref, wu_ref, wd_ref):
    xn = _rms(h, g, NORM_EPS).astype(BF16)
    d_ff = wg_ref.shape[1]
    fc = d_ff // FFN_CHUNKS
    acc = None
    for c in range(FFN_CHUNKS):
        a = jnp.dot(xn, wg_ref[:, c * fc:(c + 1) * fc], preferred_element_type=F32)
        b = jnp.dot(xn, wu_ref[:, c * fc:(c + 1) * fc], preferred_element_type=F32)
        mid = (a / (1.0 + jnp.exp(-a)) * b).astype(BF16)
        part = jnp.dot(mid, wd_ref[c * fc:(c + 1) * fc, :], preferred_element_type=F32)
        acc = part if acc is None else acc + part
    return h + 0.5 * acc


def _rope(y, cos_t, sin_t):
    w = y.shape[1]
    n = w // LANES
    c = jnp.concatenate([cos_t] * n, axis=1) if n > 1 else cos_t
    s = jnp.concatenate([sin_t] * n, axis=1) if n > 1 else sin_t
    lane = lax.broadcasted_iota(I32, y.shape, 1)
    first_half = (lane & (HEAD_DIM // 2)) == 0
    partner = jnp.where(first_half,
                        pltpu.roll(y, w - HEAD_DIM // 2, 1),
                        pltpu.roll(y, HEAD_DIM // 2, 1))
    return y * c + partner * s


def _pre_even_kernel(h_ref, ga_ref, wg_ref, wu_ref, wd_ref, gm_ref, win_ref, cos_ref, sin_ref,
                     ha_ref, qa_ref, qi_ref, qb_ref, kb_ref, ka_ref, ki_ref, va_ref, vb_ref, wi_ref,
                     *, idx_scale):
    ha = _ffn_half(h_ref[...], ga_ref[...], wg_ref, wu_ref, wd_ref)
    ha_ref[...] = ha
    xn = _rms(ha, gm_ref[...], NORM_EPS).astype(BF16)
    y = jnp.dot(xn, win_ref[...], preferred_element_type=F32)
    n_rope = 4 * 512 + 128
    yr = _rope(y[:, :n_rope], cos_ref[...], sin_ref[...])
    q_scale = ATTN_Q_SCALE
    for hd in range(8):
        lo = hd * HEAD_DIM
        qa_ref[hd] = (yr[:, lo:lo + HEAD_DIM] * q_scale).astype(BF16)
        qi_ref[hd] = yr[:, 512 + lo:512 + lo + HEAD_DIM].astype(BF16)
        qb_ref[hd] = (yr[:, 1024 + lo:1024 + lo + HEAD_DIM] * q_scale).astype(BF16)
        kb_ref[hd] = yr[:, 1536 + lo:1536 + lo + HEAD_DIM].astype(BF16)
    ka_ref[...] = yr[:, 2048:2112].astype(BF16)
    ki_ref[...] = yr[:, 2112:2176].astype(BF16)
    vb_ref[...] = y[:, 2176:2688].astype(BF16)
    va_ref[...] = y[:, 2688:2752].astype(BF16)
    wi_ref[...] = y[:, 2752:2760] * idx_scale


def _pre_odd_kernel(h_ref, ga_ref, wg_ref, wu_ref, wd_ref, gm_ref, win_ref, cos_ref, sin_ref,
                    ha_ref, q_ref, k_ref, v_ref):
    ha = _ffn_half(h_ref[...], ga_ref[...], wg_ref, wu_ref, wd_ref)
    ha_ref[...] = ha
    xn = _rms(ha, gm_ref[...], NORM_EPS).astype(BF16)
    y = jnp.dot(xn, win_ref[...], preferred_element_type=F32)
    d = q_ref.shape[1]
    yr = _rope(y[:, :2 * d], cos_ref[...], sin_ref[...])
    q_ref[...] = yr[:, :d] * ATTN_Q_SCALE
    k_ref[...] = yr[:, d:2 * d]
    v_ref[...] = y[:, 2 * d:3 * d]


def _token_specs(tm, d, seq_tiles):
    row = lambda w: pl.BlockSpec((tm, w), lambda i: (i, 0))
    rope = pl.BlockSpec((tm, LANES), lambda i: (i % seq_tiles, 0))
    return row, rope


def _pre_even(h, ga, wg, wu, wd, gm, win, cos_t, sin_t, *, batch, seq):
    m, d = h.shape
    tm = TOKEN_TILE
    nt = seq // tm
    row, rope = _token_specs(tm, d, nt)
    hm = lambda: pl.BlockSpec((None, 8, tm, HEAD_DIM), lambda i: (i // nt, 0, i % nt, 0))
    hm_shape = jax.ShapeDtypeStruct((batch, 8, seq, HEAD_DIM), BF16)
    idx_scale = (HEAD_DIM ** -0.5) * (IDX_HEADS ** -0.5)
    return pl.pallas_call(
        functools.partial(_pre_even_kernel, idx_scale=idx_scale),
        grid=(m // tm,),
        in_specs=[row(d), _resident(ga.shape), _resident(wg.shape), _resident(wu.shape),
                  _resident(wd.shape), _resident(gm.shape), _resident(win.shape), rope, rope],
        out_specs=[row(d), hm(), hm(), hm(), hm(), row(HEAD_DIM), row(HEAD_DIM), row(HEAD_DIM),
                   row(512), row(IDX_HEADS)],
        out_shape=[jax.ShapeDtypeStruct((m, d), F32), hm_shape, hm_shape, hm_shape, hm_shape,
                   jax.ShapeDtypeStruct((m, HEAD_DIM), BF16), jax.ShapeDtypeStruct((m, HEAD_DIM), BF16),
                   jax.ShapeDtypeStruct((m, HEAD_DIM), BF16), jax.ShapeDtypeStruct((m, 512), BF16),
                   jax.ShapeDtypeStruct((m, IDX_HEADS), F32)],
        compiler_params=_cparams(1),
        name="pre_even",
    )(h, ga, wg, wu, wd, gm, win, cos_t, sin_t)


def _pre_odd(h, ga, wg, wu, wd, gm, win, cos_t, sin_t, *, seq):
    m, d = h.shape
    tm = TOKEN_TILE
    nt = seq // tm
    row, rope = _token_specs(tm, d, nt)
    out = jax.ShapeDtypeStruct((m, d), F32)
    return pl.pallas_call(
        _pre_odd_kernel,
        grid=(m // tm,),
        in_specs=[row(d), _resident(ga.shape), _resident(wg.shape), _resident(wu.shape),
                  _resident(wd.shape), _resident(gm.shape), _resident(win.shape), rope, rope],
        out_specs=[row(d), row(d), row(d), row(d)],
        out_shape=[out, out, out, out],
        compiler_params=_cparams(1),
        name="pre_odd",
    )(h, ga, wg, wu, wd, gm, win, cos_t, sin_t)


def _post_kernel(*refs, n_mix, final):
    h_ref = refs[0]
    mix_refs = refs[1:1 + n_mix]
    (wout_ref, gb_ref, wg_ref, wu_ref, wd_ref, gp_ref, pg_ref, p_ref, pp_ref) = refs[1 + n_mix:10 + n_mix]
    rest = refs[10 + n_mix:]
    gf_ref, o_ref = (rest[0], rest[1]) if final else (None, rest[0])
    mix = [r[...] for r in mix_refs]
    mix = jnp.concatenate(mix, axis=1) if n_mix > 1 else mix[0]
    h1 = h_ref[...] + jnp.dot(mix, wout_ref[...], preferred_element_type=F32)
    h2 = _ffn_half(h1, gb_ref[...], wg_ref, wu_ref, wd_ref)
    z = jnp.dot(_rms(h2, gp_ref[...], NORM_EPS).astype(BF16), pg_ref[...], preferred_element_type=F32)
    gate = 1.0 / (1.0 + jnp.exp(-z))
    emb = jnp.dot(p_ref[...].astype(BF16), pp_ref[...], preferred_element_type=F32)
    h3 = h2 + gate * emb
    if final:
        h3 = _rms(h3, gf_ref[...], NORM_EPS)
    o_ref[...] = h3


def _post(h, mixes, wout, gb, wg, wu, wd, gp, pg, p, pp, gf):
    m, d = h.shape
    tm = TOKEN_TILE
    row = lambda w: pl.BlockSpec((tm, w), lambda i: (i, 0))
    final = gf is not None
    args = [h, *mixes, wout, gb, wg, wu, wd, gp, pg, p, pp]
    specs = [row(d), *[row(x.shape[1]) for x in mixes], _resident(wout.shape), _resident(gb.shape),
             _resident(wg.shape), _resident(wu.shape), _resident(wd.shape), _resident(gp.shape),
             _resident(pg.shape), row(p.shape[1]), _resident(pp.shape)]
    if final:
        args.append(gf)
        specs.append(_resident(gf.shape))
    return pl.pallas_call(
        functools.partial(_post_kernel, n_mix=len(mixes), final=final),
        grid=(m // tm,),
        in_specs=specs,
        out_specs=row(d),
        out_shape=jax.ShapeDtypeStruct((m, d), F32),
        compiler_params=_cparams(1),
        name="post",
    )(*args)


def _row_count(mask):
    return jnp.sum(jnp.where(mask, 1.0, 0.0), axis=1, keepdims=True)


def _key_to_float(key):
    key = jnp.clip(key, jnp.int32(KEY_NEG_INF), jnp.int32(KEY_POS_INF))
    return pltpu.bitcast(jnp.where(key < 0, key ^ jnp.int32(0x7FFFFFFF), key), F32)


def _topk_bias(score, causal, val_scr, idx_scr, *, top_k):
    tq, s = score.shape
    col = lax.broadcasted_iota(I32, (tq, s), 1)
    val_scr[...] = jnp.where(causal, score, -jnp.inf)

    def value_step(it, lo):
        cand = lo ^ lax.shift_left(jnp.int32(1), jnp.int32(31) - it)
        cnt = _row_count(val_scr[...] >= _key_to_float(cand))
        return jnp.where(cnt >= top_k, cand, lo)

    lo = lax.fori_loop(0, 32, value_step, jnp.full((tq, 1), INT_MIN, I32))
    thr = _key_to_float(lo)
    n_ge = _row_count(val_scr[...] >= thr)
    excess = jnp.max(n_ge) > top_k

    @pl.when(jnp.logical_not(excess))
    def _():
        val_scr[...] = jnp.where(val_scr[...] >= thr, 0.0, NEG_INF)

    @pl.when(excess)
    def _():
        vals = val_scr[...]
        above = vals > thr
        need = top_k - _row_count(above)
        idx_scr[...] = jnp.where(vals == thr, col, jnp.int32(IDX_BIG))
        n_bits = max(1, int(math.ceil(math.log2(s))))

        def index_step(it, p):
            cand = p + lax.shift_left(jnp.int32(1), jnp.int32(n_bits - 1) - it)
            cnt = _row_count(idx_scr[...] < cand)
            return jnp.where(cnt < need, cand, p)

        p = lax.fori_loop(0, n_bits, index_step, jnp.zeros((tq, 1), I32))
        keep_tie = idx_scr[...] <= p
        keep = jnp.where(above, 0.0, jnp.where(keep_tie, 0.0, NEG_INF))
        val_scr[...] = jnp.where(causal, keep, NEG_INF)

    return val_scr[...]


def _softmax_pv(s, v):
    m = jnp.max(s, axis=1, keepdims=True)
    p = jnp.exp2(s - m)
    l = jnp.sum(p, axis=1, keepdims=True)
    return jnp.dot(p.astype(BF16), v, preferred_element_type=F32) * (1.0 / l)


def _dsa_kernel(qa_ref, qi_ref, wi_ref, ka_ref, va_ref, ki_ref, o_ref, *scratch, s_len, q_start, top_k):
    tq = o_ref.shape[0]
    ki = ki_ref[...]
    w = wi_ref[...]
    row = lax.broadcasted_iota(I32, (tq, s_len), 0) + q_start
    col = lax.broadcasted_iota(I32, (tq, s_len), 1)
    causal = col <= row
    if s_len > top_k:
        score = jnp.zeros((tq, s_len), F32)
        for hd in range(IDX_HEADS):
            dots = lax.dot_general(qi_ref[hd], ki, _CONTRACT_LAST, preferred_element_type=F32)
            score = score + w[:, hd:hd + 1] * jnp.maximum(dots, 0.0)
        bias = _topk_bias(score, causal, scratch[0], scratch[1], top_k=top_k)
    else:
        bias = jnp.where(causal, 0.0, NEG_INF)
    ka = ka_ref[...]
    va = va_ref[...]
    outs = []
    for hd in range(A_HEADS):
        s = lax.dot_general(qa_ref[hd], ka, _CONTRACT_LAST, preferred_element_type=F32) + bias
        outs.append(_softmax_pv(s, va))
    o_ref[...] = jnp.concatenate(outs, axis=1).astype(BF16)


def _dsa_bucket(qa, qi, wi, ka, va, ki, *, j, top_k):
    batch, _, seq, _ = qa.shape
    tq = ATTN_TQ
    s_len = (j + 1) * tq
    hm = pl.BlockSpec((None, 8, tq, HEAD_DIM), lambda b: (b, 0, j, 0))
    kv = pl.BlockSpec((None, s_len, HEAD_DIM), lambda b: (b, 0, 0))
    scratch = [pltpu.VMEM((tq, s_len), F32), pltpu.VMEM((tq, s_len), I32)] if s_len > top_k else []
    return pl.pallas_call(
        functools.partial(_dsa_kernel, s_len=s_len, q_start=j * tq, top_k=top_k),
        grid=(batch,),
        in_specs=[hm, hm, pl.BlockSpec((None, tq, IDX_HEADS), lambda b: (b, j, 0)), kv, kv, kv],
        out_specs=pl.BlockSpec((None, tq, A_HEADS * HEAD_DIM), lambda b: (b, 0, 0)),
        out_shape=jax.ShapeDtypeStruct((batch, tq, A_HEADS * HEAD_DIM), BF16),
        scratch_shapes=scratch,
        compiler_params=_cparams(1),
        name=f"dsa_{j}",
    )(qa, qi, wi, ka, va, ki)


def _causal_attend(q, k_ref, v_ref, c, j, tri_bias):
    tq = ATTN_TQ
    off = j * tq
    s_d = lax.dot_general(q, k_ref[c, off:off + tq, :], _CONTRACT_LAST, preferred_element_type=F32) + tri_bias
    m = jnp.max(s_d, axis=1, keepdims=True)
    if j > 0:
        s_o = lax.dot_general(q, k_ref[c, 0:off, :], _CONTRACT_LAST, preferred_element_type=F32)
        m = jnp.maximum(m, jnp.max(s_o, axis=1, keepdims=True))
    p_d = jnp.exp2(s_d - m)
    l = jnp.sum(p_d, axis=1, keepdims=True)
    o = jnp.dot(p_d.astype(BF16), v_ref[off:off + tq, :], preferred_element_type=F32)
    if j > 0:
        p_o = jnp.exp2(s_o - m)
        l = l + jnp.sum(p_o, axis=1, keepdims=True)
        o = o + jnp.dot(p_o.astype(BF16), v_ref[0:off, :], preferred_element_type=F32)
    return o * (1.0 / l)


def _diff_kernel(lam_ref, subln_ref, q_ref, k_ref, v_ref, o_ref, *, lambda_init):
    seq = o_ref.shape[0]
    tq = ATTN_TQ
    lv = lam_ref[...]
    lam = (jnp.exp(jnp.sum(lv[0:1] * lv[1:2], axis=1, keepdims=True))
           - jnp.exp(jnp.sum(lv[2:3] * lv[3:4], axis=1, keepdims=True)) + lambda_init)
    row = lax.broadcasted_iota(I32, (tq, tq), 0)
    col = lax.broadcasted_iota(I32, (tq, tq), 1)
    tri_bias = jnp.where(col <= row, 0.0, NEG_INF)
    subln = subln_ref[...]
    for j in range(seq // tq):
        rows = slice(j * tq, (j + 1) * tq)
        a = (_causal_attend(q_ref[0, rows, :], k_ref, v_ref, 0, j, tri_bias)
             - lam * _causal_attend(q_ref[1, rows, :], k_ref, v_ref, 1, j, tri_bias))
        y = _rms(a, subln, SUBLN_EPS) * (1.0 - lambda_init)
        o_ref[rows, :] = y.astype(BF16)


def _diff_attention(lam_vecs, subln, qb, kb, vb, *, lambda_init):
    batch, _, seq, _ = qb.shape
    qk = lambda: pl.BlockSpec((None, 2, seq, HEAD_DIM), lambda b, h: (b, h, 0, 0))
    return pl.pallas_call(
        functools.partial(_diff_kernel, lambda_init=lambda_init),
        grid=(batch, B_HEADS),
        in_specs=[pl.BlockSpec(lam_vecs.shape, lambda b, h: (0, 0)),
                  pl.BlockSpec(subln.shape, lambda b, h: (0, 0)),
                  qk(), qk(),
                  pl.BlockSpec((None, seq, B_VDIM), lambda b, h: (b, 0, h))],
        out_specs=pl.BlockSpec((None, seq, B_VDIM), lambda b, h: (b, 0, h)),
        out_shape=jax.ShapeDtypeStruct((batch, seq, B_HEADS * B_VDIM), BF16),
        compiler_params=_cparams(2),
        name="diff",
    )(lam_vecs, subln, qb, kb, vb)


def _unroll_for(trips):
    return max(u for u in range(1, DIL_MAX_UNROLL + 1) if trips % u == 0)


def _dil_kernel(q_ref, k_ref, v_ref, o_ref, m_scr, l_scr, acc_scr):
    seq = q_ref.shape[0]
    span = DIL_SPAN
    lane = lax.broadcasted_iota(I32, (span, LANES), 1)
    head0 = lane < HEAD_DIM

    def rows_at(start, d):
        return pl.ds(start, span) if d == 1 else pl.ds(start, span, stride=d)

    def tile(t, *, d, blk0, first):
        r = t % d
        blk = blk0 + t // d
        start = r + blk * (span * d)
        if d == 1:
            start = pl.multiple_of(start, span)
        rows = rows_at(start, d)
        qb = q_ref[rows, :]
        qs = jnp.concatenate([jnp.where(head0, qb, 0.0), jnp.where(head0, 0.0, qb)], axis=0).astype(BF16)
        kk = k_ref[rows, :]
        vv = v_ref[rows, :]
        has_prev = blk0 > 0
        if has_prev:
            prev = rows_at(start - span * d, d)
            kk = jnp.concatenate([k_ref[prev, :], kk], axis=0)
            vv = jnp.concatenate([v_ref[prev, :], vv], axis=0)
        nk = kk.shape[0]
        s = lax.dot_general(qs, kk.astype(BF16), _CONTRACT_LAST, preferred_element_type=F32)
        qi = lax.broadcasted_iota(I32, (2 * span, nk), 0) & (span - 1)
        kj = lax.broadcasted_iota(I32, (2 * span, nk), 1)
        if has_prev:
            dist = kj - qi
            ok = jnp.where(kj < span, dist, span - dist) >= 0
        else:
            ok = kj <= qi
        s = jnp.where(ok, s, NEG_INF)
        m = jnp.max(s, axis=1, keepdims=True)
        p = jnp.exp2(s - m)
        l = jnp.sum(p, axis=1, keepdims=True)
        pv = jnp.dot(p.astype(BF16), vv.astype(BF16), preferred_element_type=F32)
        acc_t = jnp.where(head0, pv[:span], pv[span:])
        m_t = jnp.where(head0, m[:span], m[span:])
        l_t = jnp.where(head0, l[:span], l[span:])
        if first:
            m_scr[rows, :] = m_t
            l_scr[rows, :] = l_t
            acc_scr[rows, :] = acc_t
        else:
            m_o = m_scr[rows, :]
            m_n = jnp.maximum(m_o, m_t)
            ea = jnp.exp2(m_o - m_n)
            eb = jnp.exp2(m_t - m_n)
            l_scr[rows, :] = ea * l_scr[rows, :] + eb * l_t
            acc_scr[rows, :] = ea * acc_scr[rows, :] + eb * acc_t
            m_scr[rows, :] = m_n

    for pi, (window, d) in enumerate(reversed(C_PATTERNS)):
        n_blocks = (seq // d) // span
        first = pi == 0

        def no_prev(t, carry, d=d, first=first):
            tile(t, d=d, blk0=0, first=first)
            return carry

        def with_prev(t, carry, d=d, first=first):
            tile(t, d=d, blk0=1, first=first)
            return carry

        lax.fori_loop(0, d, no_prev, 0, unroll=_unroll_for(d))
        if n_blocks > 1:
            trips = d * (n_blocks - 1)
            lax.fori_loop(0, trips, with_prev, 0, unroll=_unroll_for(trips))

    o_ref[...] = (acc_scr[...] * (1.0 / l_scr[...])).astype(BF16)


def _dilated(q, k, v):
    batch, seq, d = q.shape
    n_pairs = d // LANES
    blk = lambda: pl.BlockSpec((None, seq, LANES), lambda b, hp: (b, 0, hp))
    return pl.pallas_call(
        _dil_kernel,
        grid=(batch, n_pairs),
        in_specs=[blk(), blk(), blk()],
        out_specs=blk(),
        out_shape=jax.ShapeDtypeStruct((batch, seq, d), BF16),
        scratch_shapes=[pltpu.VMEM((seq, LANES), F32)] * 3,
        compiler_params=_cparams(2),
        name="dilated",
    )(q, k, v)


def _rope_tables(seq):
    inv = 1.0 / (ROPE_THETA ** (jnp.arange(0, HEAD_DIM, 2, dtype=F32) / HEAD_DIM))
    ang = jnp.arange(seq, dtype=F32)[:, None] * inv[None, :]
    cos, sin = jnp.cos(ang), jnp.sin(ang)
    cos_t = jnp.concatenate([cos, cos, cos, cos], axis=1)
    sin_t = jnp.concatenate([-sin, sin, -sin, sin], axis=1)
    return cos_t, sin_t


def _pack_even_w_in(w):
    cuts = np.cumsum([0, 512, 64, 64, 512, 64, 8, 512, 512, 512])
    qa, ka, va, qi, ki, wi, qb, kb, vb = [w[:, cuts[i]:cuts[i + 1]] for i in range(9)]
    packed = jnp.concatenate([qa, qi, qb, kb, ka, ki, vb, va, wi], axis=1)
    pad = (-packed.shape[1]) % LANES
    return jnp.pad(packed, ((0, 0), (0, pad))).astype(BF16)


def kernel(x, p, norm_ffn_a, ffn_a_wg, ffn_a_wu, ffn_a_wd, norm_mix, norm_ffn_b, ffn_b_wg, ffn_b_wu,
           ffn_b_wd, norm_ple, ple_gate, ple_proj, even_w_in, even_w_out, diff_lambda_q1,
           diff_lambda_k1, diff_lambda_q2, diff_lambda_k2, diff_subln, odd_w_in, odd_w_out, final_norm):
    batch, seq, d_model = x.shape
    depth = p.shape[0]
    assert seq % ATTN_TQ == 0 and (batch * seq) % TOKEN_TILE == 0 and seq % TOKEN_TILE == 0
    assert seq == C_PATTERNS[-1][0], "dilated kernel assumes the widest window spans the sequence"
    top_k = min(A_TOPK_MAX, seq // 4)
    cos_t, sin_t = _rope_tables(seq)
    gain = lambda g: g.reshape(1, -1).astype(F32)
    bf = lambda w: w.astype(BF16)
    n_buckets = seq // ATTN_TQ

    h = x.reshape(batch * seq, d_model)
    for i in range(depth):
        pre_args = (h, gain(norm_ffn_a[i]), bf(ffn_a_wg[i]), bf(ffn_a_wu[i]), bf(ffn_a_wd[i]),
                    gain(norm_mix[i]))
        if i % 2 == 0:
            e = i // 2
            lambda_init = 0.8 - 0.6 * math.exp(-0.3 * i)
            (ha, qa, qi, qb, kb, ka, ki, va, vb, wi) = _pre_even(
                *pre_args, _pack_even_w_in(even_w_in[e]), cos_t, sin_t, batch=batch, seq=seq)
            r3 = lambda a: a.reshape(batch, seq, a.shape[-1])
            ka, ki, va, vb, wi = r3(ka), r3(ki), r3(va), r3(vb), r3(wi)
            lam_vecs = jnp.stack([diff_lambda_q1[e], diff_lambda_k1[e],
                                  diff_lambda_q2[e], diff_lambda_k2[e]]).astype(F32)
            subln = gain(diff_subln[e])
            out_a = [_dsa_bucket(qa, qi, wi, ka, va, ki, j=j, top_k=top_k) for j in range(n_buckets)]
            out_b = _diff_attention(lam_vecs, subln, qb, kb, vb, lambda_init=lambda_init)
            mixes = [jnp.concatenate(out_a, axis=1).reshape(batch * seq, -1),
                     out_b.reshape(batch * seq, -1)]
            w_out = bf(even_w_out[e])
        else:
            o = i // 2
            ha, q, k, v = _pre_odd(*pre_args, bf(odd_w_in[o]), cos_t, sin_t, seq=seq)
            r3 = lambda a: a.reshape(batch, seq, a.shape[-1])
            mixes = [_dilated(r3(q), r3(k), r3(v)).reshape(batch * seq, -1)]
            w_out = bf(odd_w_out[o])
        gf = gain(final_norm) if i == depth - 1 else None
        h = _post(ha, mixes, w_out, gain(norm_ffn_b[i]), bf(ffn_b_wg[i]), bf(ffn_b_wu[i]),
                  bf(ffn_b_wd[i]), gain(norm_ple[i]), bf(ple_gate[i]),
                  p[i].reshape(batch * seq, -1), bf(ple_proj[i]), gf)
    return h.reshape(batch, seq, d_model)
```

```python
import functools
import math

import jax
import jax.numpy as jnp
import numpy as np
from jax import lax
from jax.experimental import pallas as pl
from jax.experimental.pallas import tpu as pltpu

F32 = jnp.float32
BF16 = jnp.bfloat16
I32 = jnp.int32
I16 = jnp.int16

HEAD_DIM = 64
ROPE_THETA = 10000.0
NORM_EPS = 1e-6
SUBLN_EPS = 1e-5
NEG_INF = -1e30
A_HEADS = 8
A_TOPK_MAX = 256
IDX_HEADS = 8
B_HEADS = 4
B_VDIM = 2 * HEAD_DIM
C_PATTERNS = ((128, 1), (512, 4), (2048, 16))
DIL_SPAN = 128

LANES = 128
BF16_ROWS = 16
VMEM_LIMIT_BYTES = 56 * 1024 * 1024
TOKEN_TILE = 256
FFN_CHUNKS = 2
ATTN_TQ = 256
DIL_MAX_UNROLL = 5

ATTN_Q_SCALE = (HEAD_DIM ** -0.5) * math.log2(math.e)

INT_MIN = -(2 ** 31)
KEY_NEG_INF = INT_MIN + 0x7FFFFF
KEY_POS_INF = 0x7F800000
IDX_BIG = 2 ** 30

EVEN_ROPE_COLS = 4 * 512 + 2 * HEAD_DIM
EVEN_T_ROWS = HEAD_DIM + IDX_HEADS

_CONTRACT_LAST = (((1,), (1,)), ((), ()))
_CONTRACT_FIRST = (((0,), (0,)), ((), ()))


def _cparams(n_grid):
    return pltpu.CompilerParams(dimension_semantics=("arbitrary",) * n_grid,
                                vmem_limit_bytes=VMEM_LIMIT_BYTES)


def _resident(shape):
    nd = len(shape)
    return pl.BlockSpec(shape, lambda *_: (0,) * nd, pipeline_mode=pl.Buffered(1))


def _rms(x, g, eps):
    ms = jnp.mean(x * x, axis=-1, keepdims=True)
    return x * lax.rsqrt(ms + eps) * g


def _ffn_half(h, g, wg_ref, wu_ref, wd_ref):
    xn = _rms(h, g, NORM_EPS).astype(BF16)
    d_ff = wg_ref.shape[1]
    fc = d_ff // FFN_CHUNKS
    acc = None
    for c in range(FFN_CHUNKS):
        a = jnp.dot(xn, wg_ref[:, c * fc:(c + 1) * fc], preferred_element_type=F32)
        b = jnp.dot(xn, wu_ref[:, c * fc:(c + 1) * fc], preferred_element_type=F32)
        mid = (a / (1.0 + jnp.exp(-a)) * b).astype(BF16)
        part = jnp.dot(mid, wd_ref[c * fc:(c + 1) * fc, :], preferred_element_type=F32)
        acc = part if acc is None else acc + part
    return h + 0.5 * acc


def _rope(y, cos_t, sin_t):
    w = y.shape[1]
    n = w // LANES
    c = jnp.concatenate([cos_t] * n, axis=1) if n > 1 else cos_t
    s = jnp.concatenate([sin_t] * n, axis=1) if n > 1 else sin_t
    lane = lax.broadcasted_iota(I32, y.shape, 1)
    first_half = (lane & (HEAD_DIM // 2)) == 0
    partner = jnp.where(first_half,
                        pltpu.roll(y, w - HEAD_DIM // 2, 1),
                        pltpu.roll(y, HEAD_DIM // 2, 1))
    return y * c + partner * s


def _pre_even_kernel(h_ref, ga_ref, wg_ref, wu_ref, wd_ref, gm_ref, wr_ref, wt_ref, cos_ref, sin_ref,
                     ha_ref, qa_ref, qi_ref, qb_ref, kb_ref, ka_ref, ki_ref, vb_ref, vat_ref, wit_ref,
                     *, idx_scale):
    ha = _ffn_half(h_ref[...], ga_ref[...], wg_ref, wu_ref, wd_ref)
    ha_ref[...] = ha
    xn = _rms(ha, gm_ref[...], NORM_EPS).astype(BF16)
    y = jnp.dot(xn, wr_ref[...], preferred_element_type=F32)
    yr = _rope(y[:, :EVEN_ROPE_COLS], cos_ref[...], sin_ref[...])
    vb_ref[...] = y[:, EVEN_ROPE_COLS:EVEN_ROPE_COLS + B_HEADS * B_VDIM].astype(BF16)
    for hd in range(8):
        lo = hd * HEAD_DIM
        qa_ref[hd] = (yr[:, lo:lo + HEAD_DIM] * ATTN_Q_SCALE).astype(BF16)
        qi_ref[hd] = yr[:, 512 + lo:512 + lo + HEAD_DIM].astype(BF16)
        qb_ref[hd] = (yr[:, 1024 + lo:1024 + lo + HEAD_DIM] * ATTN_Q_SCALE).astype(BF16)
        kb_ref[hd] = yr[:, 1536 + lo:1536 + lo + HEAD_DIM].astype(BF16)
    ka_ref[...] = yr[:, 2048:2112].astype(BF16)
    ki_ref[...] = yr[:, 2112:2176].astype(BF16)
    yt = lax.dot_general(wt_ref[...], xn, _CONTRACT_LAST, preferred_element_type=F32)
    vat_ref[...] = yt[0:HEAD_DIM].astype(BF16)
    wit_ref[...] = yt[HEAD_DIM:HEAD_DIM + IDX_HEADS] * idx_scale


def _pre_odd_kernel(h_ref, ga_ref, wg_ref, wu_ref, wd_ref, gm_ref, win_ref, cos_ref, sin_ref,
                    ha_ref, q_ref, k_ref, v_ref):
    ha = _ffn_half(h_ref[...], ga_ref[...], wg_ref, wu_ref, wd_ref)
    ha_ref[...] = ha
    xn = _rms(ha, gm_ref[...], NORM_EPS).astype(BF16)
    y = jnp.dot(xn, win_ref[...], preferred_element_type=F32)
    d = q_ref.shape[1]
    yr = _rope(y[:, :2 * d], cos_ref[...], sin_ref[...])
    q_ref[...] = yr[:, :d] * ATTN_Q_SCALE
    k_ref[...] = yr[:, d:2 * d]
    v_ref[...] = y[:, 2 * d:3 * d]


def _token_specs(tm, seq_tiles):
    row = lambda w: pl.BlockSpec((tm, w), lambda i: (i, 0))
    rope = pl.BlockSpec((tm, LANES), lambda i: (i % seq_tiles, 0))
    return row, rope


def _pre_even(h, ga, wg, wu, wd, gm, w_rope, w_t, cos_t, sin_t, *, batch, seq):
    m, d = h.shape
    tm = TOKEN_TILE
    nt = seq // tm
    row, rope = _token_specs(tm, nt)
    hm = lambda: pl.BlockSpec((None, 8, tm, HEAD_DIM), lambda i: (i // nt, 0, i % nt, 0))
    hm_shape = jax.ShapeDtypeStruct((batch, 8, seq, HEAD_DIM), BF16)
    tr = lambda rows: pl.BlockSpec((None, rows, tm), lambda i: (i // nt, 0, i % nt))
    idx_scale = (HEAD_DIM ** -0.5) * (IDX_HEADS ** -0.5)
    return pl.pallas_call(
        functools.partial(_pre_even_kernel, idx_scale=idx_scale),
        grid=(m // tm,),
        in_specs=[row(d), _resident(ga.shape), _resident(wg.shape), _resident(wu.shape),
                  _resident(wd.shape), _resident(gm.shape), _resident(w_rope.shape),
                  _resident(w_t.shape), rope, rope],
        out_specs=[row(d), hm(), hm(), hm(), hm(), row(HEAD_DIM), row(HEAD_DIM),
                   row(B_HEADS * B_VDIM), tr(HEAD_DIM), tr(IDX_HEADS)],
        out_shape=[jax.ShapeDtypeStruct((m, d), F32), hm_shape, hm_shape, hm_shape, hm_shape,
                   jax.ShapeDtypeStruct((m, HEAD_DIM), BF16), jax.ShapeDtypeStruct((m, HEAD_DIM), BF16),
                   jax.ShapeDtypeStruct((m, B_HEADS * B_VDIM), BF16),
                   jax.ShapeDtypeStruct((batch, HEAD_DIM, seq), BF16),
                   jax.ShapeDtypeStruct((batch, IDX_HEADS, seq), F32)],
        compiler_params=_cparams(1),
        name="pre_even",
    )(h, ga, wg, wu, wd, gm, w_rope, w_t, cos_t, sin_t)


def _pre_odd(h, ga, wg, wu, wd, gm, win, cos_t, sin_t, *, seq):
    m, d = h.shape
    tm = TOKEN_TILE
    nt = seq // tm
    row, rope = _token_specs(tm, nt)
    out = jax.ShapeDtypeStruct((m, d), F32)
    return pl.pallas_call(
        _pre_odd_kernel,
        grid=(m // tm,),
        in_specs=[row(d), _resident(ga.shape), _resident(wg.shape), _resident(wu.shape),
                  _resident(wd.shape), _resident(gm.shape), _resident(win.shape), rope, rope],
        out_specs=[row(d), row(d), row(d), row(d)],
        out_shape=[out, out, out, out],
        compiler_params=_cparams(1),
        name="pre_odd",
    )(h, ga, wg, wu, wd, gm, win, cos_t, sin_t)


def _post_kernel(*refs, n_mix, mix_transposed, final):
    h_ref = refs[0]
    mix_refs = refs[1:1 + n_mix]
    (wout_ref, gb_ref, wg_ref, wu_ref, wd_ref, gp_ref, pg_ref, p_ref, pp_ref) = refs[1 + n_mix:10 + n_mix]
    rest = refs[10 + n_mix:]
    gf_ref, o_ref = (rest[0], rest[1]) if final else (None, rest[0])
    h1 = h_ref[...]
    row0 = 0
    for r, transposed in zip(mix_refs, mix_transposed):
        if transposed:
            w = wout_ref[row0:row0 + r.shape[0], :]
            h1 = h1 + lax.dot_general(r[...], w, _CONTRACT_FIRST, preferred_element_type=F32)
            row0 += r.shape[0]
        else:
            w = wout_ref[row0:row0 + r.shape[1], :]
            h1 = h1 + jnp.dot(r[...], w, preferred_element_type=F32)
            row0 += r.shape[1]
    h2 = _ffn_half(h1, gb_ref[...], wg_ref, wu_ref, wd_ref)
    z = jnp.dot(_rms(h2, gp_ref[...], NORM_EPS).astype(BF16), pg_ref[...], preferred_element_type=F32)
    gate = 1.0 / (1.0 + jnp.exp(-z))
    emb = jnp.dot(p_ref[...].astype(BF16), pp_ref[...], preferred_element_type=F32)
    h3 = h2 + gate * emb
    if final:
        h3 = _rms(h3, gf_ref[...], NORM_EPS)
    o_ref[...] = h3


def _post(h, mixes, wout, gb, wg, wu, wd, gp, pg, p, pp, gf, *, mix_transposed, seq):
    m, d = h.shape
    tm = TOKEN_TILE
    nt = seq // tm
    row = lambda w: pl.BlockSpec((tm, w), lambda i: (i, 0))
    mix_specs = [pl.BlockSpec((None, x.shape[1], tm), lambda i: (i // nt, 0, i % nt)) if t else row(x.shape[1])
                 for x, t in zip(mixes, mix_transposed)]
    final = gf is not None
    args = [h, *mixes, wout, gb, wg, wu, wd, gp, pg, p, pp]
    specs = [row(d), *mix_specs, _resident(wout.shape), _resident(gb.shape),
             _resident(wg.shape), _resident(wu.shape), _resident(wd.shape), _resident(gp.shape),
             _resident(pg.shape), row(p.shape[1]), _resident(pp.shape)]
    if final:
        args.append(gf)
        specs.append(_resident(gf.shape))
    return pl.pallas_call(
        functools.partial(_post_kernel, n_mix=len(mixes), mix_transposed=mix_transposed, final=final),
        grid=(m // tm,),
        in_specs=specs,
        out_specs=row(d),
        out_shape=jax.ShapeDtypeStruct((m, d), F32),
        compiler_params=_cparams(1),
        name="post",
    )(*args)


def _key_count(mask):
    return jnp.sum(jnp.where(mask, 1.0, 0.0), axis=0, keepdims=True)


def _key_to_float(key):
    key = jnp.clip(key, jnp.int32(KEY_NEG_INF), jnp.int32(KEY_POS_INF))
    return pltpu.bitcast(jnp.where(key < 0, key ^ jnp.int32(0x7FFFFFFF), key), F32)


def _count16(mask):
    s = mask.shape[0]
    ones = jnp.where(mask, jnp.ones((), I16), jnp.zeros((), I16))
    parts = [ones[i:i + BF16_ROWS] for i in range(0, s, BF16_ROWS)]
    while len(parts) > 1:
        nxt = [parts[i] + parts[i + 1] for i in range(0, len(parts) - 1, 2)]
        parts = nxt + ([parts[-1]] if len(parts) % 2 else [])
    return jnp.sum(parts[0].astype(F32), axis=0, keepdims=True)


def _kth_largest_fast(val_scr, hi_scr, lo_scr, *, top_k):
    tq = val_scr.shape[1]
    half_min = jnp.int32(-(2 ** 15))

    def keys():
        bits = pltpu.bitcast(val_scr[...], I32)
        return jnp.where(bits < 0, bits ^ jnp.int32(0x7FFFFFFF), bits)

    hi_scr[...] = lax.shift_right_arithmetic(keys(), jnp.int32(16)).astype(I16)

    def search(ref, need):
        def step(it, acc):
            cand = acc + lax.shift_left(jnp.int32(1), jnp.int32(15) - it)
            cnt = _count16(ref[...] >= cand.astype(I16))
            return jnp.where(cnt >= need, cand, acc)
        return lax.fori_loop(0, 16, step, jnp.full((1, tq), half_min, I32))

    hi = search(hi_scr, top_k)
    hi16 = hi.astype(I16)
    need = top_k - _count16(hi_scr[...] > hi16)
    low = ((keys() & jnp.int32(0xFFFF)) + half_min).astype(I16)
    lo_scr[...] = jnp.where(hi_scr[...] == hi16, low, half_min.astype(I16))
    lo = search(lo_scr, need)
    return _key_to_float(lax.shift_left(hi, jnp.int32(16)) | ((lo - half_min) & jnp.int32(0xFFFF)))


def _kth_largest_float(val_scr, *, top_k):
    tq = val_scr.shape[1]

    def value_step(it, lo):
        cand = lo ^ lax.shift_left(jnp.int32(1), jnp.int32(31) - it)
        cnt = _key_count(val_scr[...] >= _key_to_float(cand))
        return jnp.where(cnt >= top_k, cand, lo)

    return _key_to_float(lax.fori_loop(0, 32, value_step, jnp.full((1, tq), INT_MIN, I32)))


def _topk_bias(score, causal, val_scr, idx_scr, hi_scr, lo_scr, thr_scr, *, top_k):
    s, tq = score.shape
    key_idx = lax.broadcasted_iota(I32, (s, tq), 0)
    val_scr[...] = jnp.where(causal, score, -jnp.inf)

    thr_scr[...] = jnp.broadcast_to(_kth_largest_fast(val_scr, hi_scr, lo_scr, top_k=top_k), thr_scr.shape)
    thr0 = thr_scr[0:1, :]
    verified = jnp.logical_and(jnp.min(_key_count(val_scr[...] >= thr0)) >= top_k,
                               jnp.max(_key_count(val_scr[...] > thr0)) < top_k)

    @pl.when(jnp.logical_not(verified))
    def _():
        thr_scr[...] = jnp.broadcast_to(_kth_largest_float(val_scr, top_k=top_k), thr_scr.shape)

    thr = thr_scr[0:1, :]
    n_ge = _key_count(val_scr[...] >= thr)
    excess = jnp.max(n_ge) > top_k

    @pl.when(jnp.logical_not(excess))
    def _():
        val_scr[...] = jnp.where(val_scr[...] >= thr, 0.0, NEG_INF)

    @pl.when(excess)
    def _():
        vals = val_scr[...]
        above = vals > thr
        need = top_k - _key_count(above)
        idx_scr[...] = jnp.where(vals == thr, key_idx, jnp.int32(IDX_BIG))
        n_bits = max(1, int(math.ceil(math.log2(s))))

        def index_step(it, p):
            cand = p + lax.shift_left(jnp.int32(1), jnp.int32(n_bits - 1) - it)
            cnt = _key_count(idx_scr[...] < cand)
            return jnp.where(cnt < need, cand, p)

        p = lax.fori_loop(0, n_bits, index_step, jnp.zeros((1, tq), I32))
        keep_tie = idx_scr[...] <= p
        keep = jnp.where(above, 0.0, jnp.where(keep_tie, 0.0, NEG_INF))
        val_scr[...] = jnp.where(causal, keep, NEG_INF)

    return val_scr[...]


def _with_ones_rows(v_t):
    return jnp.concatenate([v_t, jnp.ones((BF16_ROWS, v_t.shape[1]), BF16)], axis=0)


def _dsa_kernel(qa_ref, qi_ref, wit_ref, ka_ref, vat_ref, ki_ref, o_ref, *scratch, s_len, q_start, top_k):
    tq = o_ref.shape[1]
    key_idx = lax.broadcasted_iota(I32, (s_len, tq), 0)
    q_pos = lax.broadcasted_iota(I32, (s_len, tq), 1) + q_start
    causal = key_idx <= q_pos
    if s_len > top_k:
        ki = ki_ref[...]
        w_t = wit_ref[...]
        score = jnp.zeros((s_len, tq), F32)
        for hd in range(IDX_HEADS):
            dots = lax.dot_general(ki, qi_ref[hd], _CONTRACT_LAST, preferred_element_type=F32)
            score = score + w_t[hd:hd + 1, :] * jnp.maximum(dots, 0.0)
        bias = _topk_bias(score, causal, *scratch, top_k=top_k)
    else:
        bias = jnp.where(causal, 0.0, NEG_INF)
    ka = ka_ref[...]
    v_ext = _with_ones_rows(vat_ref[...])
    for hd in range(A_HEADS):
        s_t = lax.dot_general(ka, qa_ref[hd], _CONTRACT_LAST, preferred_element_type=F32) + bias
        m = jnp.max(s_t, axis=0, keepdims=True)
        p = jnp.exp2(s_t - m).astype(BF16)
        r = jnp.dot(v_ext, p, preferred_element_type=F32)
        o = r[0:HEAD_DIM] * (1.0 / r[HEAD_DIM:HEAD_DIM + 1])
        o_ref[hd * HEAD_DIM:(hd + 1) * HEAD_DIM, :] = o.astype(BF16)


def _dsa_bucket(qa, qi, wi_t, ka, va_t, ki, *, j, top_k):
    batch, _, seq, _ = qa.shape
    tq = ATTN_TQ
    s_len = (j + 1) * tq
    hm = pl.BlockSpec((None, 8, tq, HEAD_DIM), lambda b: (b, 0, j, 0))
    kv = pl.BlockSpec((None, s_len, HEAD_DIM), lambda b: (b, 0, 0))
    scratch = [pltpu.VMEM((s_len, tq), F32), pltpu.VMEM((s_len, tq), I32), pltpu.VMEM((s_len, tq), I16),
               pltpu.VMEM((s_len, tq), I16), pltpu.VMEM((8, tq), F32)] if s_len > top_k else []
    return pl.pallas_call(
        functools.partial(_dsa_kernel, s_len=s_len, q_start=j * tq, top_k=top_k),
        grid=(batch,),
        in_specs=[hm, hm, pl.BlockSpec((None, IDX_HEADS, tq), lambda b: (b, 0, j)), kv,
                  pl.BlockSpec((None, HEAD_DIM, s_len), lambda b: (b, 0, 0)), kv],
        out_specs=pl.BlockSpec((None, A_HEADS * HEAD_DIM, tq), lambda b: (b, 0, 0)),
        out_shape=jax.ShapeDtypeStruct((batch, A_HEADS * HEAD_DIM, tq), BF16),
        scratch_shapes=scratch,
        compiler_params=_cparams(1),
        name=f"dsa_{j}",
    )(qa, qi, wi_t, ka, va_t, ki)


def _causal_attend(q, k_ref, v_ext, c, j, tri_bias):
    tq = ATTN_TQ
    off = j * tq
    s_d = lax.dot_general(q, k_ref[c, off:off + tq, :], _CONTRACT_LAST, preferred_element_type=F32) + tri_bias
    m = jnp.max(s_d, axis=1, keepdims=True)
    if j > 0:
        s_o = lax.dot_general(q, k_ref[c, 0:off, :], _CONTRACT_LAST, preferred_element_type=F32)
        m = jnp.maximum(m, jnp.max(s_o, axis=1, keepdims=True))
    r = jnp.dot(jnp.exp2(s_d - m).astype(BF16), v_ext[off:off + tq, :], preferred_element_type=F32)
    if j > 0:
        r = r + jnp.dot(jnp.exp2(s_o - m).astype(BF16), v_ext[0:off, :], preferred_element_type=F32)
    return r[:, :B_VDIM] * (1.0 / r[:, B_VDIM:])


def _diff_kernel(lam_ref, subln_ref, q_ref, k_ref, v_ref, o_ref, *, lambda_init):
    seq = o_ref.shape[0]
    tq = ATTN_TQ
    lv = lam_ref[...]
    lam = (jnp.exp(jnp.sum(lv[0:1] * lv[1:2], axis=1, keepdims=True))
           - jnp.exp(jnp.sum(lv[2:3] * lv[3:4], axis=1, keepdims=True)) + lambda_init)
    row = lax.broadcasted_iota(I32, (tq, tq), 0)
    col = lax.broadcasted_iota(I32, (tq, tq), 1)
    tri_bias = jnp.where(col <= row, 0.0, NEG_INF)
    subln = subln_ref[...]
    v = v_ref[...]
    v_ext = jnp.concatenate([v, jnp.ones(v.shape, BF16)], axis=1)
    for j in range(seq // tq):
        rows = slice(j * tq, (j + 1) * tq)
        a = (_causal_attend(q_ref[0, rows, :], k_ref, v_ext, 0, j, tri_bias)
             - lam * _causal_attend(q_ref[1, rows, :], k_ref, v_ext, 1, j, tri_bias))
        y = _rms(a, subln, SUBLN_EPS) * (1.0 - lambda_init)
        o_ref[rows, :] = y.astype(BF16)


def _diff_attention(lam_vecs, subln, qb, kb, vb, *, lambda_init):
    batch, _, seq, _ = qb.shape
    qk = lambda: pl.BlockSpec((None, 2, seq, HEAD_DIM), lambda b, h: (b, h, 0, 0))
    vo = lambda: pl.BlockSpec((None, seq, B_VDIM), lambda b, h: (b, 0, h))
    return pl.pallas_call(
        functools.partial(_diff_kernel, lambda_init=lambda_init),
        grid=(batch, B_HEADS),
        in_specs=[pl.BlockSpec(lam_vecs.shape, lambda b, h: (0, 0)),
                  pl.BlockSpec(subln.shape, lambda b, h: (0, 0)),
                  qk(), qk(), vo()],
        out_specs=vo(),
        out_shape=jax.ShapeDtypeStruct((batch, seq, B_HEADS * B_VDIM), BF16),
        compiler_params=_cparams(2),
        name="diff",
    )(lam_vecs, subln, qb, kb, vb)


def _unroll_for(trips):
    return max(u for u in range(1, DIL_MAX_UNROLL + 1) if trips % u == 0)


def _dil_kernel(q_ref, k_ref, v_ref, o_ref, m_scr, l_scr, acc_scr, bias_scr):
    seq = q_ref.shape[0]
    span = DIL_SPAN
    lane = lax.broadcasted_iota(I32, (span, LANES), 1)
    head0 = lane < HEAD_DIM

    qi = lax.broadcasted_iota(I32, (2 * span, 2 * span), 0) & (span - 1)
    kj = lax.broadcasted_iota(I32, (2 * span, 2 * span), 1)
    dist = kj - qi
    bias_scr[...] = jnp.where(jnp.where(kj < span, dist, span - dist) >= 0, 0.0, NEG_INF)

    def rows_at(start, d):
        return pl.ds(start, span) if d == 1 else pl.ds(start, span, stride=d)

    def tile(t, *, d, blk0, first):
        r = t % d
        blk = blk0 + t // d
        start = r + blk * (span * d)
        if d == 1:
            start = pl.multiple_of(start, span)
        rows = rows_at(start, d)
        qb = q_ref[rows, :]
        qs = jnp.concatenate([jnp.where(head0, qb, 0.0), jnp.where(head0, 0.0, qb)], axis=0).astype(BF16)
        kk = k_ref[rows, :]
        vv = v_ref[rows, :]
        has_prev = blk0 > 0
        if has_prev:
            prev = rows_at(start - span * d, d)
            kk = jnp.concatenate([k_ref[prev, :], kk], axis=0)
            vv = jnp.concatenate([v_ref[prev, :], vv], axis=0)
            bias = bias_scr[...]
        else:
            bias = bias_scr[:, span:]
        s = lax.dot_general(qs, kk.astype(BF16), _CONTRACT_LAST, preferred_element_type=F32) + bias
        m = jnp.max(s, axis=1, keepdims=True)
        p = jnp.exp2(s - m).astype(BF16)
        v_ext = jnp.concatenate([vv.astype(BF16), jnp.ones(vv.shape, BF16)], axis=1)
        pv = jnp.dot(p, v_ext, preferred_element_type=F32)
        acc_t = jnp.where(head0, pv[:span, :LANES], pv[span:, :LANES])
        l_t = jnp.where(head0, pv[:span, LANES:], pv[span:, LANES:])
        m_t = jnp.where(head0, m[:span], m[span:])
        if first:
            m_scr[rows, :] = m_t
            l_scr[rows, :] = l_t
            acc_scr[rows, :] = acc_t
        else:
            m_o = m_scr[rows, :]
            m_n = jnp.maximum(m_o, m_t)
            ea = jnp.exp2(m_o - m_n)
            eb = jnp.exp2(m_t - m_n)
            l_scr[rows, :] = ea * l_scr[rows, :] + eb * l_t
            acc_scr[rows, :] = ea * acc_scr[rows, :] + eb * acc_t
            m_scr[rows, :] = m_n

    for pi, (window, d) in enumerate(reversed(C_PATTERNS)):
        n_blocks = (seq // d) // span
        first = pi == 0

        def no_prev(t, carry, d=d, first=first):
            tile(t, d=d, blk0=0, first=first)
            return carry

        def with_prev(t, carry, d=d, first=first):
            tile(t, d=d, blk0=1, first=first)
            return carry

        lax.fori_loop(0, d, no_prev, 0, unroll=_unroll_for(d))
        if n_blocks > 1:
            trips = d * (n_blocks - 1)
            lax.fori_loop(0, trips, with_prev, 0, unroll=_unroll_for(trips))

    o_ref[...] = (acc_scr[...] * (1.0 / l_scr[...])).astype(BF16)


def _dilated(q, k, v):
    batch, seq, d = q.shape
    n_pairs = d // LANES
    blk = lambda: pl.BlockSpec((None, seq, LANES), lambda b, hp: (b, 0, hp))
    return pl.pallas_call(
        _dil_kernel,
        grid=(batch, n_pairs),
        in_specs=[blk(), blk(), blk()],
        out_specs=blk(),
        out_shape=jax.ShapeDtypeStruct((batch, seq, d), BF16),
        scratch_shapes=[pltpu.VMEM((seq, LANES), F32)] * 3 + [pltpu.VMEM((2 * DIL_SPAN, 2 * DIL_SPAN), F32)],
        compiler_params=_cparams(2),
        name="dilated",
    )(q, k, v)


def _rope_tables(seq):
    inv = 1.0 / (ROPE_THETA ** (jnp.arange(0, HEAD_DIM, 2, dtype=F32) / HEAD_DIM))
    ang = jnp.arange(seq, dtype=F32)[:, None] * inv[None, :]
    cos, sin = jnp.cos(ang), jnp.sin(ang)
    cos_t = jnp.concatenate([cos, cos, cos, cos], axis=1)
    sin_t = jnp.concatenate([-sin, sin, -sin, sin], axis=1)
    return cos_t, sin_t


def _pack_even_w_in(w):
    cuts = np.cumsum([0, 512, 64, 64, 512, 64, 8, 512, 512, 512])
    qa, ka, va, qi, ki, wi, qb, kb, vb = [w[:, cuts[i]:cuts[i + 1]] for i in range(9)]
    w_cols = jnp.concatenate([qa, qi, qb, kb, ka, ki, vb], axis=1).astype(BF16)
    w_t = jnp.concatenate([va, wi], axis=1).T
    pad = (-w_t.shape[0]) % BF16_ROWS
    w_t = jnp.pad(w_t, ((0, pad), (0, 0))).astype(BF16)
    assert w_cols.shape[1] == EVEN_ROPE_COLS + B_HEADS * B_VDIM and w_t.shape[0] >= EVEN_T_ROWS
    return w_cols, w_t


def kernel(x, p, norm_ffn_a, ffn_a_wg, ffn_a_wu, ffn_a_wd, norm_mix, norm_ffn_b, ffn_b_wg, ffn_b_wu,
           ffn_b_wd, norm_ple, ple_gate, ple_proj, even_w_in, even_w_out, diff_lambda_q1,
           diff_lambda_k1, diff_lambda_q2, diff_lambda_k2, diff_subln, odd_w_in, odd_w_out, final_norm):
    batch, seq, d_model = x.shape
    depth = p.shape[0]
    assert seq % ATTN_TQ == 0 and (batch * seq) % TOKEN_TILE == 0 and seq % TOKEN_TILE == 0
    assert seq == C_PATTERNS[-1][0], "dilated kernel assumes the widest window spans the sequence"
    top_k = min(A_TOPK_MAX, seq // 4)
    cos_t, sin_t = _rope_tables(seq)
    gain = lambda g: g.reshape(1, -1).astype(F32)
    bf = lambda w: w.astype(BF16)
    n_buckets = seq // ATTN_TQ
    r3 = lambda a: a.reshape(batch, seq, a.shape[-1])

    h = x.reshape(batch * seq, d_model)
    for i in range(depth):
        pre_args = (h, gain(norm_ffn_a[i]), bf(ffn_a_wg[i]), bf(ffn_a_wu[i]), bf(ffn_a_wd[i]),
                    gain(norm_mix[i]))
        if i % 2 == 0:
            e = i // 2
            lambda_init = 0.8 - 0.6 * math.exp(-0.3 * i)
            (ha, qa, qi, qb, kb, ka, ki, vb, va_t, wi_t) = _pre_even(
                *pre_args, *_pack_even_w_in(even_w_in[e]), cos_t, sin_t, batch=batch, seq=seq)
            ka, ki, vb = r3(ka), r3(ki), r3(vb)
            lam_vecs = jnp.stack([diff_lambda_q1[e], diff_lambda_k1[e],
                                  diff_lambda_q2[e], diff_lambda_k2[e]]).astype(F32)
            out_a = [_dsa_bucket(qa, qi, wi_t, ka, va_t, ki, j=j, top_k=top_k) for j in range(n_buckets)]
            out_b = _diff_attention(lam_vecs, gain(diff_subln[e]), qb, kb, vb, lambda_init=lambda_init)
            mixes = [jnp.concatenate(out_a, axis=2), out_b.reshape(batch * seq, -1)]
            mix_transposed = (True, False)
            w_out = bf(even_w_out[e])
        else:
            o = i // 2
            ha, q, k, v = _pre_odd(*pre_args, bf(odd_w_in[o]), cos_t, sin_t, seq=seq)
            mixes = [_dilated(r3(q), r3(k), r3(v)).reshape(batch * seq, -1)]
            mix_transposed = (False,)
            w_out = bf(odd_w_out[o])
        gf = gain(final_norm) if i == depth - 1 else None
        h = _post(ha, mixes, w_out, gain(norm_ffn_b[i]), bf(ffn_b_wg[i]), bf(ffn_b_wu[i]),
                  bf(ffn_b_wd[i]), gain(norm_ple[i]), bf(ple_gate[i]),
                  p[i].reshape(batch * seq, -1), bf(ple_proj[i]), gf,
                  mix_transposed=mix_transposed, seq=seq)
    return h.reshape(batch, seq, d_model)
```

```python
import functools
import math

import jax
import jax.numpy as jnp
import numpy as np
from jax import lax
from jax.experimental import pallas as pl
from jax.experimental.pallas import tpu as pltpu

F32 = jnp.float32
BF16 = jnp.bfloat16
I32 = jnp.int32
I16 = jnp.int16

HEAD_DIM = 64
ROPE_THETA = 10000.0
NORM_EPS = 1e-6
SUBLN_EPS = 1e-5
NEG_INF = -1e30
A_HEADS = 8
A_TOPK_MAX = 256
IDX_HEADS = 8
B_HEADS = 4
B_VDIM = 2 * HEAD_DIM
C_PATTERNS = ((128, 1), (512, 4), (2048, 16))
DIL_SPAN = 128

LANES = 128
BF16_ROWS = 16
VMEM_LIMIT_BYTES = 56 * 1024 * 1024
TOKEN_TILE = 256
FFN_CHUNKS = 2
ATTN_TQ = 256
DIL_MAX_UNROLL = 5
WEIGHT_STAGE_ELEMS = 192 * 1024

ATTN_Q_SCALE = (HEAD_DIM ** -0.5) * math.log2(math.e)

INT_MIN = -(2 ** 31)
KEY_NEG_INF = INT_MIN + 0x7FFFFF
KEY_POS_INF = 0x7F800000
IDX_BIG = 2 ** 30

EVEN_ROPE_COLS = 4 * 512 + 2 * HEAD_DIM
EVEN_T_ROWS = HEAD_DIM + IDX_HEADS

_CONTRACT_LAST = (((1,), (1,)), ((), ()))
_CONTRACT_FIRST = (((0,), (0,)), ((), ()))


def _cparams(n_grid):
    return pltpu.CompilerParams(dimension_semantics=("arbitrary",) * n_grid,
                                vmem_limit_bytes=VMEM_LIMIT_BYTES)


def _resident(shape):
    nd = len(shape)
    return pl.BlockSpec(shape, lambda *_: (0,) * nd, pipeline_mode=pl.Buffered(1))


_HBM = pl.BlockSpec(memory_space=pl.ANY)


def _stage_rows(rows, cols):
    return max(r for r in range(BF16_ROWS, rows + 1, BF16_ROWS)
               if rows % r == 0 and r * cols <= WEIGHT_STAGE_ELEMS)


def _weight_scratch(shapes):
    stage = (2, max(_stage_rows(r, c) for r, c in shapes), max(c for _, c in shapes))
    return [pltpu.VMEM(s, BF16) for s in shapes] + [pltpu.VMEM(stage, F32), pltpu.SemaphoreType.DMA((2,))]


def _fetch_weights(layer, hbm_refs, vmem_refs, stage, sem):
    @pl.when(pl.program_id(0) == 0)
    def _():
        for w_hbm, w_vmem in zip(hbm_refs, vmem_refs):
            rows, cols = w_vmem.shape
            rc = _stage_rows(rows, cols)
            n = rows // rc

            def copy(c, w_hbm=w_hbm, rc=rc, cols=cols):
                slot = c % 2
                return pltpu.make_async_copy(w_hbm.at[layer, pl.ds(c * rc, rc), :],
                                             stage.at[slot, pl.ds(0, rc), pl.ds(0, cols)], sem.at[slot])

            def body(c, carry, copy=copy, w_vmem=w_vmem, rc=rc, cols=cols, n=n):
                @pl.when(c + 1 < n)
                def _():
                    copy(c + 1).start()
                copy(c).wait()
                w_vmem[pl.ds(pl.multiple_of(c * rc, rc), rc), :] = stage[c % 2, 0:rc, 0:cols].astype(BF16)
                return carry

            copy(0).start()
            lax.fori_loop(0, n, body, 0)


def _rms(x, g, eps):
    ms = jnp.mean(x * x, axis=-1, keepdims=True)
    return x * lax.rsqrt(ms + eps) * g


def _ffn_half(h, g, wg_ref, wu_ref, wd_ref):
    xn = _rms(h, g, NORM_EPS).astype(BF16)
    d_ff = wg_ref.shape[1]
    fc = d_ff // FFN_CHUNKS
    acc = None
    for c in range(FFN_CHUNKS):
        a = jnp.dot(xn, wg_ref[:, c * fc:(c + 1) * fc], preferred_element_type=F32)
        b = jnp.dot(xn, wu_ref[:, c * fc:(c + 1) * fc], preferred_element_type=F32)
        mid = (a / (1.0 + jnp.exp(-a)) * b).astype(BF16)
        part = jnp.dot(mid, wd_ref[c * fc:(c + 1) * fc, :], preferred_element_type=F32)
        acc = part if acc is None else acc + part
    return h + 0.5 * acc


def _rope(y, cos_t, sin_t):
    w = y.shape[1]
    n = w // LANES
    c = jnp.concatenate([cos_t] * n, axis=1) if n > 1 else cos_t
    s = jnp.concatenate([sin_t] * n, axis=1) if n > 1 else sin_t
    lane = lax.broadcasted_iota(I32, y.shape, 1)
    first_half = (lane & (HEAD_DIM // 2)) == 0
    partner = jnp.where(first_half,
                        pltpu.roll(y, w - HEAD_DIM // 2, 1),
                        pltpu.roll(y, HEAD_DIM // 2, 1))
    return y * c + partner * s


def _pre_even_kernel(h_ref, ga_ref, gm_ref, wr_ref, wt_ref, cos_ref, sin_ref, wg_hbm, wu_hbm, wd_hbm,
                     ha_ref, qa_ref, qi_ref, qb_ref, kb_ref, ka_ref, ki_ref, vb_ref, vat_ref, wit_ref,
                     wg_ref, wu_ref, wd_ref, stage, sem, *, layer, idx_scale):
    _fetch_weights(layer, (wg_hbm, wu_hbm, wd_hbm), (wg_ref, wu_ref, wd_ref), stage, sem)
    ha = _ffn_half(h_ref[...], ga_ref[...], wg_ref, wu_ref, wd_ref)
    ha_ref[...] = ha
    xn = _rms(ha, gm_ref[...], NORM_EPS).astype(BF16)
    y = jnp.dot(xn, wr_ref[...], preferred_element_type=F32)
    yr = _rope(y[:, :EVEN_ROPE_COLS], cos_ref[...], sin_ref[...])
    vb_ref[...] = y[:, EVEN_ROPE_COLS:EVEN_ROPE_COLS + B_HEADS * B_VDIM].astype(BF16)
    for hd in range(8):
        lo = hd * HEAD_DIM
        qa_ref[hd] = (yr[:, lo:lo + HEAD_DIM] * ATTN_Q_SCALE).astype(BF16)
        qi_ref[hd] = yr[:, 512 + lo:512 + lo + HEAD_DIM].astype(BF16)
        qb_ref[hd] = (yr[:, 1024 + lo:1024 + lo + HEAD_DIM] * ATTN_Q_SCALE).astype(BF16)
        kb_ref[hd] = yr[:, 1536 + lo:1536 + lo + HEAD_DIM].astype(BF16)
    ka_ref[...] = yr[:, 2048:2112].astype(BF16)
    ki_ref[...] = yr[:, 2112:2176].astype(BF16)
    yt = lax.dot_general(wt_ref[...], xn, _CONTRACT_LAST, preferred_element_type=F32)
    vat_ref[...] = yt[0:HEAD_DIM].astype(BF16)
    wit_ref[...] = yt[HEAD_DIM:HEAD_DIM + IDX_HEADS] * idx_scale


def _pre_odd_kernel(h_ref, ga_ref, gm_ref, cos_ref, sin_ref, wg_hbm, wu_hbm, wd_hbm, win_hbm,
                    ha_ref, q_ref, k_ref, v_ref,
                    wg_ref, wu_ref, wd_ref, win_ref, stage, sem, *, layer, mixer_layer):
    _fetch_weights(layer, (wg_hbm, wu_hbm, wd_hbm), (wg_ref, wu_ref, wd_ref), stage, sem)
    _fetch_weights(mixer_layer, (win_hbm,), (win_ref,), stage, sem)
    ha = _ffn_half(h_ref[...], ga_ref[...], wg_ref, wu_ref, wd_ref)
    ha_ref[...] = ha
    xn = _rms(ha, gm_ref[...], NORM_EPS).astype(BF16)
    y = jnp.dot(xn, win_ref[...], preferred_element_type=F32)
    d = q_ref.shape[1]
    yr = _rope(y[:, :2 * d], cos_ref[...], sin_ref[...])
    q_ref[...] = yr[:, :d] * ATTN_Q_SCALE
    k_ref[...] = yr[:, d:2 * d]
    v_ref[...] = y[:, 2 * d:3 * d]


def _token_specs(tm, seq_tiles):
    row = lambda w: pl.BlockSpec((tm, w), lambda i: (i, 0))
    rope = pl.BlockSpec((tm, LANES), lambda i: (i % seq_tiles, 0))
    return row, rope


def _pre_even(h, ga, gm, w_cols, w_t, cos_t, sin_t, wg_all, wu_all, wd_all, *, layer, batch, seq):
    m, d = h.shape
    tm = TOKEN_TILE
    nt = seq // tm
    row, rope = _token_specs(tm, nt)
    hm = lambda: pl.BlockSpec((None, 8, tm, HEAD_DIM), lambda i: (i // nt, 0, i % nt, 0))
    hm_shape = jax.ShapeDtypeStruct((batch, 8, seq, HEAD_DIM), BF16)
    tr = lambda rows: pl.BlockSpec((None, rows, tm), lambda i: (i // nt, 0, i % nt))
    idx_scale = (HEAD_DIM ** -0.5) * (IDX_HEADS ** -0.5)
    return pl.pallas_call(
        functools.partial(_pre_even_kernel, layer=layer, idx_scale=idx_scale),
        grid=(m // tm,),
        in_specs=[row(d), _resident(ga.shape), _resident(gm.shape), _resident(w_cols.shape),
                  _resident(w_t.shape), rope, rope, _HBM, _HBM, _HBM],
        out_specs=[row(d), hm(), hm(), hm(), hm(), row(HEAD_DIM), row(HEAD_DIM),
                   row(B_HEADS * B_VDIM), tr(HEAD_DIM), tr(IDX_HEADS)],
        out_shape=[jax.ShapeDtypeStruct((m, d), F32), hm_shape, hm_shape, hm_shape, hm_shape,
                   jax.ShapeDtypeStruct((m, HEAD_DIM), BF16), jax.ShapeDtypeStruct((m, HEAD_DIM), BF16),
                   jax.ShapeDtypeStruct((m, B_HEADS * B_VDIM), BF16),
                   jax.ShapeDtypeStruct((batch, HEAD_DIM, seq), BF16),
                   jax.ShapeDtypeStruct((batch, IDX_HEADS, seq), F32)],
        scratch_shapes=_weight_scratch([wg_all.shape[1:], wu_all.shape[1:], wd_all.shape[1:]]),
        compiler_params=_cparams(1),
        name="pre_even",
    )(h, ga, gm, w_cols, w_t, cos_t, sin_t, wg_all, wu_all, wd_all)


def _pre_odd(h, ga, gm, cos_t, sin_t, wg_all, wu_all, wd_all, win_all, *, layer, mixer_layer, seq):
    m, d = h.shape
    tm = TOKEN_TILE
    nt = seq // tm
    row, rope = _token_specs(tm, nt)
    out = jax.ShapeDtypeStruct((m, d), F32)
    return pl.pallas_call(
        functools.partial(_pre_odd_kernel, layer=layer, mixer_layer=mixer_layer),
        grid=(m // tm,),
        in_specs=[row(d), _resident(ga.shape), _resident(gm.shape), rope, rope, _HBM, _HBM, _HBM, _HBM],
        out_specs=[row(d), row(d), row(d), row(d)],
        out_shape=[out, out, out, out],
        scratch_shapes=_weight_scratch([wg_all.shape[1:], wu_all.shape[1:], wd_all.shape[1:], win_all.shape[1:]]),
        compiler_params=_cparams(1),
        name="pre_odd",
    )(h, ga, gm, cos_t, sin_t, wg_all, wu_all, wd_all, win_all)


def _post_kernel(*refs, n_mix, mix_transposed, final, layer, mixer_layer):
    refs = list(refs)
    h_ref = refs.pop(0)
    mix_refs = [refs.pop(0) for _ in range(n_mix)]
    gb_ref, gp_ref, p_ref = (refs.pop(0) for _ in range(3))
    gf_ref = refs.pop(0) if final else None
    wout_hbm, wg_hbm, wu_hbm, wd_hbm, pg_hbm, pp_hbm = (refs.pop(0) for _ in range(6))
    o_ref = refs.pop(0)
    wout_ref, wg_ref, wu_ref, wd_ref, pg_ref, pp_ref, stage, sem = refs
    _fetch_weights(mixer_layer, (wout_hbm,), (wout_ref,), stage, sem)
    _fetch_weights(layer, (wg_hbm, wu_hbm, wd_hbm, pg_hbm, pp_hbm), (wg_ref, wu_ref, wd_ref, pg_ref, pp_ref),
                   stage, sem)
    h1 = h_ref[...]
    row0 = 0
    for r, transposed in zip(mix_refs, mix_transposed):
        if transposed:
            w = wout_ref[row0:row0 + r.shape[0], :]
            h1 = h1 + lax.dot_general(r[...], w, _CONTRACT_FIRST, preferred_element_type=F32)
            row0 += r.shape[0]
        else:
            w = wout_ref[row0:row0 + r.shape[1], :]
            h1 = h1 + jnp.dot(r[...], w, preferred_element_type=F32)
            row0 += r.shape[1]
    h2 = _ffn_half(h1, gb_ref[...], wg_ref, wu_ref, wd_ref)
    z = jnp.dot(_rms(h2, gp_ref[...], NORM_EPS).astype(BF16), pg_ref[...], preferred_element_type=F32)
    gate = 1.0 / (1.0 + jnp.exp(-z))
    emb = jnp.dot(p_ref[...].astype(BF16), pp_ref[...], preferred_element_type=F32)
    h3 = h2 + gate * emb
    if final:
        h3 = _rms(h3, gf_ref[...], NORM_EPS)
    o_ref[...] = h3


def _post(h, mixes, gb, gp, p_all, gf, wout_all, wg_all, wu_all, wd_all, pg_all, pp_all,
          *, mix_transposed, layer, mixer_layer, seq):
    m, d = h.shape
    tm = TOKEN_TILE
    nt = seq // tm
    row = lambda w: pl.BlockSpec((tm, w), lambda i: (i, 0))
    mix_specs = [pl.BlockSpec((None, x.shape[1], tm), lambda i: (i // nt, 0, i % nt)) if t else row(x.shape[1])
                 for x, t in zip(mixes, mix_transposed)]
    final = gf is not None
    weights = [wout_all, wg_all, wu_all, wd_all, pg_all, pp_all]
    args = [h, *mixes, gb, gp, p_all] + ([gf] if final else []) + weights
    specs = ([row(d), *mix_specs, _resident(gb.shape), _resident(gp.shape),
              pl.BlockSpec((None, tm, p_all.shape[2]), lambda i: (layer, i, 0))]
             + ([_resident(gf.shape)] if final else []) + [_HBM] * len(weights))
    return pl.pallas_call(
        functools.partial(_post_kernel, n_mix=len(mixes), mix_transposed=mix_transposed, final=final,
                          layer=layer, mixer_layer=mixer_layer),
        grid=(m // tm,),
        in_specs=specs,
        out_specs=row(d),
        out_shape=jax.ShapeDtypeStruct((m, d), F32),
        scratch_shapes=_weight_scratch([w.shape[1:] for w in weights]),
        compiler_params=_cparams(1),
        name="post",
    )(*args)


def _key_count(mask):
    return jnp.sum(jnp.where(mask, 1.0, 0.0), axis=0, keepdims=True)


def _key_to_float(key):
    key = jnp.clip(key, jnp.int32(KEY_NEG_INF), jnp.int32(KEY_POS_INF))
    return pltpu.bitcast(jnp.where(key < 0, key ^ jnp.int32(0x7FFFFFFF), key), F32)


def _count16(mask):
    s = mask.shape[0]
    ones = jnp.where(mask, jnp.ones((), I16), jnp.zeros((), I16))
    parts = [ones[i:i + BF16_ROWS] for i in range(0, s, BF16_ROWS)]
    while len(parts) > 1:
        nxt = [parts[i] + parts[i + 1] for i in range(0, len(parts) - 1, 2)]
        parts = nxt + ([parts[-1]] if len(parts) % 2 else [])
    return jnp.sum(parts[0].astype(F32), axis=0, keepdims=True)


def _kth_largest_fast(val_scr, hi_scr, lo_scr, *, top_k):
    tq = val_scr.shape[1]
    half_min = jnp.int32(-(2 ** 15))

    def keys():
        bits = pltpu.bitcast(val_scr[...], I32)
        return jnp.where(bits < 0, bits ^ jnp.int32(0x7FFFFFFF), bits)

    hi_scr[...] = lax.shift_right_arithmetic(keys(), jnp.int32(16)).astype(I16)

    def search(ref, need):
        def step(it, acc):
            cand = acc + lax.shift_left(jnp.int32(1), jnp.int32(15) - it)
            cnt = _count16(ref[...] >= cand.astype(I16))
            return jnp.where(cnt >= need, cand, acc)
        return lax.fori_loop(0, 16, step, jnp.full((1, tq), half_min, I32))

    hi = search(hi_scr, top_k)
    hi16 = hi.astype(I16)
    need = top_k - _count16(hi_scr[...] > hi16)
    low = ((keys() & jnp.int32(0xFFFF)) + half_min).astype(I16)
    lo_scr[...] = jnp.where(hi_scr[...] == hi16, low, half_min.astype(I16))
    lo = search(lo_scr, need)
    return _key_to_float(lax.shift_left(hi, jnp.int32(16)) | ((lo - half_min) & jnp.int32(0xFFFF)))


def _kth_largest_float(val_scr, *, top_k):
    tq = val_scr.shape[1]

    def value_step(it, lo):
        cand = lo ^ lax.shift_left(jnp.int32(1), jnp.int32(31) - it)
        cnt = _key_count(val_scr[...] >= _key_to_float(cand))
        return jnp.where(cnt >= top_k, cand, lo)

    return _key_to_float(lax.fori_loop(0, 32, value_step, jnp.full((1, tq), INT_MIN, I32)))


def _topk_bias(score, causal, val_scr, idx_scr, hi_scr, lo_scr, thr_scr, *, top_k):
    s, tq = score.shape
    key_idx = lax.broadcasted_iota(I32, (s, tq), 0)
    val_scr[...] = jnp.where(causal, score, -jnp.inf)

    thr_scr[...] = jnp.broadcast_to(_kth_largest_fast(val_scr, hi_scr, lo_scr, top_k=top_k), thr_scr.shape)
    thr0 = thr_scr[0:1, :]
    verified = jnp.logical_and(jnp.min(_key_count(val_scr[...] >= thr0)) >= top_k,
                               jnp.max(_key_count(val_scr[...] > thr0)) < top_k)

    @pl.when(jnp.logical_not(verified))
    def _():
        thr_scr[...] = jnp.broadcast_to(_kth_largest_float(val_scr, top_k=top_k), thr_scr.shape)

    thr = thr_scr[0:1, :]
    n_ge = _key_count(val_scr[...] >= thr)
    excess = jnp.max(n_ge) > top_k

    @pl.when(jnp.logical_not(excess))
    def _():
        val_scr[...] = jnp.where(val_scr[...] >= thr, 0.0, NEG_INF)

    @pl.when(excess)
    def _():
        vals = val_scr[...]
        above = vals > thr
        need = top_k - _key_count(above)
        idx_scr[...] = jnp.where(vals == thr, key_idx, jnp.int32(IDX_BIG))
        n_bits = max(1, int(math.ceil(math.log2(s))))

        def index_step(it, p):
            cand = p + lax.shift_left(jnp.int32(1), jnp.int32(n_bits - 1) - it)
            cnt = _key_count(idx_scr[...] < cand)
            return jnp.where(cnt < need, cand, p)

        p = lax.fori_loop(0, n_bits, index_step, jnp.zeros((1, tq), I32))
        keep_tie = idx_scr[...] <= p
        keep = jnp.where(above, 0.0, jnp.where(keep_tie, 0.0, NEG_INF))
        val_scr[...] = jnp.where(causal, keep, NEG_INF)

    return val_scr[...]


def _with_ones_rows(v_t):
    return jnp.concatenate([v_t, jnp.ones((BF16_ROWS, v_t.shape[1]), BF16)], axis=0)


def _dsa_kernel(qa_ref, qi_ref, wit_ref, ka_ref, vat_ref, ki_ref, o_ref, *scratch, s_len, q_start, top_k):
    tq = o_ref.shape[1]
    key_idx = lax.broadcasted_iota(I32, (s_len, tq), 0)
    q_pos = lax.broadcasted_iota(I32, (s_len, tq), 1) + q_start
    causal = key_idx <= q_pos
    if s_len > top_k:
        ki = ki_ref[...]
        w_t = wit_ref[...]
        score = jnp.zeros((s_len, tq), F32)
        for hd in range(IDX_HEADS):
            dots = lax.dot_general(ki, qi_ref[hd], _CONTRACT_LAST, preferred_element_type=F32)
            score = score + w_t[hd:hd + 1, :] * jnp.maximum(dots, 0.0)
        bias = _topk_bias(score, causal, *scratch, top_k=top_k)
    else:
        bias = jnp.where(causal, 0.0, NEG_INF)
    ka = ka_ref[...]
    v_ext = _with_ones_rows(vat_ref[...])
    for hd in range(A_HEADS):
        s_t = lax.dot_general(ka, qa_ref[hd], _CONTRACT_LAST, preferred_element_type=F32) + bias
        m = jnp.max(s_t, axis=0, keepdims=True)
        p = jnp.exp2(s_t - m).astype(BF16)
        r = jnp.dot(v_ext, p, preferred_element_type=F32)
        o = r[0:HEAD_DIM] * (1.0 / r[HEAD_DIM:HEAD_DIM + 1])
        o_ref[hd * HEAD_DIM:(hd + 1) * HEAD_DIM, :] = o.astype(BF16)


def _dsa_bucket(qa, qi, wi_t, ka, va_t, ki, *, j, top_k):
    batch, _, seq, _ = qa.shape
    tq = ATTN_TQ
    s_len = (j + 1) * tq
    hm = pl.BlockSpec((None, 8, tq, HEAD_DIM), lambda b: (b, 0, j, 0))
    kv = pl.BlockSpec((None, s_len, HEAD_DIM), lambda b: (b, 0, 0))
    scratch = [pltpu.VMEM((s_len, tq), F32), pltpu.VMEM((s_len, tq), I32), pltpu.VMEM((s_len, tq), I16),
               pltpu.VMEM((s_len, tq), I16), pltpu.VMEM((8, tq), F32)] if s_len > top_k else []
    return pl.pallas_call(
        functools.partial(_dsa_kernel, s_len=s_len, q_start=j * tq, top_k=top_k),
        grid=(batch,),
        in_specs=[hm, hm, pl.BlockSpec((None, IDX_HEADS, tq), lambda b: (b, 0, j)), kv,
                  pl.BlockSpec((None, HEAD_DIM, s_len), lambda b: (b, 0, 0)), kv],
        out_specs=pl.BlockSpec((None, A_HEADS * HEAD_DIM, tq), lambda b: (b, 0, 0)),
        out_shape=jax.ShapeDtypeStruct((batch, A_HEADS * HEAD_DIM, tq), BF16),
        scratch_shapes=scratch,
        compiler_params=_cparams(1),
        name=f"dsa_{j}",
    )(qa, qi, wi_t, ka, va_t, ki)


def _causal_attend(q, k_ref, v_ext, c, j, tri_bias):
    tq = ATTN_TQ
    off = j * tq
    s_d = lax.dot_general(q, k_ref[c, off:off + tq, :], _CONTRACT_LAST, preferred_element_type=F32) + tri_bias
    m = jnp.max(s_d, axis=1, keepdims=True)
    if j > 0:
        s_o = lax.dot_general(q, k_ref[c, 0:off, :], _CONTRACT_LAST, preferred_element_type=F32)
        m = jnp.maximum(m, jnp.max(s_o, axis=1, keepdims=True))
    r = jnp.dot(jnp.exp2(s_d - m).astype(BF16), v_ext[off:off + tq, :], preferred_element_type=F32)
    if j > 0:
        r = r + jnp.dot(jnp.exp2(s_o - m).astype(BF16), v_ext[0:off, :], preferred_element_type=F32)
    return r[:, :B_VDIM] * (1.0 / r[:, B_VDIM:])


def _diff_kernel(lam_ref, subln_ref, q_ref, k_ref, v_ref, o_ref, *, lambda_init):
    seq = o_ref.shape[0]
    tq = ATTN_TQ
    lv = lam_ref[...]
    lam = (jnp.exp(jnp.sum(lv[0:1] * lv[1:2], axis=1, keepdims=True))
           - jnp.exp(jnp.sum(lv[2:3] * lv[3:4], axis=1, keepdims=True)) + lambda_init)
    row = lax.broadcasted_iota(I32, (tq, tq), 0)
    col = lax.broadcasted_iota(I32, (tq, tq), 1)
    tri_bias = jnp.where(col <= row, 0.0, NEG_INF)
    subln = subln_ref[...]
    v = v_ref[...]
    v_ext = jnp.concatenate([v, jnp.ones(v.shape, BF16)], axis=1)
    for j in range(seq // tq):
        rows = slice(j * tq, (j + 1) * tq)
        a = (_causal_attend(q_ref[0, rows, :], k_ref, v_ext, 0, j, tri_bias)
             - lam * _causal_attend(q_ref[1, rows, :], k_ref, v_ext, 1, j, tri_bias))
        y = _rms(a, subln, SUBLN_EPS) * (1.0 - lambda_init)
        o_ref[rows, :] = y.astype(BF16)


def _diff_attention(lam_vecs, subln, qb, kb, vb, *, lambda_init):
    batch, _, seq, _ = qb.shape
    qk = lambda: pl.BlockSpec((None, 2, seq, HEAD_DIM), lambda b, h: (b, h, 0, 0))
    vo = lambda: pl.BlockSpec((None, seq, B_VDIM), lambda b, h: (b, 0, h))
    return pl.pallas_call(
        functools.partial(_diff_kernel, lambda_init=lambda_init),
        grid=(batch, B_HEADS),
        in_specs=[pl.BlockSpec(lam_vecs.shape, lambda b, h: (0, 0)),
                  pl.BlockSpec(subln.shape, lambda b, h: (0, 0)),
                  qk(), qk(), vo()],
        out_specs=vo(),
        out_shape=jax.ShapeDtypeStruct((batch, seq, B_HEADS * B_VDIM), BF16),
        compiler_params=_cparams(2),
        name="diff",
    )(lam_vecs, subln, qb, kb, vb)


def _unroll_for(trips):
    return max(u for u in range(1, DIL_MAX_UNROLL + 1) if trips % u == 0)


def _dil_kernel(q_ref, k_ref, v_ref, o_ref, m_scr, l_scr, acc_scr, bias_scr):
    seq = q_ref.shape[0]
    span = DIL_SPAN
    lane = lax.broadcasted_iota(I32, (span, LANES), 1)
    head0 = lane < HEAD_DIM

    qi = lax.broadcasted_iota(I32, (2 * span, 2 * span), 0) & (span - 1)
    kj = lax.broadcasted_iota(I32, (2 * span, 2 * span), 1)
    dist = kj - qi
    bias_scr[...] = jnp.where(jnp.where(kj < span, dist, span - dist) >= 0, 0.0, NEG_INF)

    def rows_at(start, d):
        return pl.ds(start, span) if d == 1 else pl.ds(start, span, stride=d)

    def tile(t, *, d, blk0, first):
        r = t % d
        blk = blk0 + t // d
        start = r + blk * (span * d)
        if d == 1:
            start = pl.multiple_of(start, span)
        rows = rows_at(start, d)
        qb = q_ref[rows, :]
        qs = jnp.concatenate([jnp.where(head0, qb, 0.0), jnp.where(head0, 0.0, qb)], axis=0).astype(BF16)
        kk = k_ref[rows, :]
        vv = v_ref[rows, :]
        has_prev = blk0 > 0
        if has_prev:
            prev = rows_at(start - span * d, d)
            kk = jnp.concatenate([k_ref[prev, :], kk], axis=0)
            vv = jnp.concatenate([v_ref[prev, :], vv], axis=0)
            bias = bias_scr[...]
        else:
            bias = bias_scr[:, span:]
        s = lax.dot_general(qs, kk.astype(BF16), _CONTRACT_LAST, preferred_element_type=F32) + bias
        m = jnp.max(s, axis=1, keepdims=True)
        p = jnp.exp2(s - m).astype(BF16)
        v_ext = jnp.concatenate([vv.astype(BF16), jnp.ones(vv.shape, BF16)], axis=1)
        pv = jnp.dot(p, v_ext, preferred_element_type=F32)
        acc_t = jnp.where(head0, pv[:span, :LANES], pv[span:, :LANES])
        l_t = jnp.where(head0, pv[:span, LANES:], pv[span:, LANES:])
        m_t = jnp.where(head0, m[:span], m[span:])
        if first:
            m_scr[rows, :] = m_t
            l_scr[rows, :] = l_t
            acc_scr[rows, :] = acc_t
        else:
            m_o = m_scr[rows, :]
            m_n = jnp.maximum(m_o, m_t)
            ea = jnp.exp2(m_o - m_n)
            eb = jnp.exp2(m_t - m_n)
            l_scr[rows, :] = ea * l_scr[rows, :] + eb * l_t
            acc_scr[rows, :] = ea * acc_scr[rows, :] + eb * acc_t
            m_scr[rows, :] = m_n

    for pi, (window, d) in enumerate(reversed(C_PATTERNS)):
        n_blocks = (seq // d) // span
        first = pi == 0

        def no_prev(t, carry, d=d, first=first):
            tile(t, d=d, blk0=0, first=first)
            return carry

        def with_prev(t, carry, d=d, first=first):
            tile(t, d=d, blk0=1, first=first)
            return carry

        lax.fori_loop(0, d, no_prev, 0, unroll=_unroll_for(d))
        if n_blocks > 1:
            trips = d * (n_blocks - 1)
            lax.fori_loop(0, trips, with_prev, 0, unroll=_unroll_for(trips))

    o_ref[...] = (acc_scr[...] * (1.0 / l_scr[...])).astype(BF16)


def _dilated(q, k, v):
    batch, seq, d = q.shape
    n_pairs = d // LANES
    blk = lambda: pl.BlockSpec((None, seq, LANES), lambda b, hp: (b, 0, hp))
    return pl.pallas_call(
        _dil_kernel,
        grid=(batch, n_pairs),
        in_specs=[blk(), blk(), blk()],
        out_specs=blk(),
        out_shape=jax.ShapeDtypeStruct((batch, seq, d), BF16),
        scratch_shapes=[pltpu.VMEM((seq, LANES), F32)] * 3 + [pltpu.VMEM((2 * DIL_SPAN, 2 * DIL_SPAN), F32)],
        compiler_params=_cparams(2),
        name="dilated",
    )(q, k, v)


def _rope_tables(seq):
    inv = 1.0 / (ROPE_THETA ** (jnp.arange(0, HEAD_DIM, 2, dtype=F32) / HEAD_DIM))
    ang = jnp.arange(seq, dtype=F32)[:, None] * inv[None, :]
    cos, sin = jnp.cos(ang), jnp.sin(ang)
    cos_t = jnp.concatenate([cos, cos, cos, cos], axis=1)
    sin_t = jnp.concatenate([-sin, sin, -sin, sin], axis=1)
    return cos_t, sin_t


def _pack_even_w_in(w):
    cuts = np.cumsum([0, 512, 64, 64, 512, 64, 8, 512, 512, 512])
    qa, ka, va, qi, ki, wi, qb, kb, vb = [w[:, cuts[i]:cuts[i + 1]] for i in range(9)]
    w_cols = jnp.concatenate([qa, qi, qb, kb, ka, ki, vb], axis=1).astype(BF16)
    w_t = jnp.concatenate([va, wi], axis=1).T
    pad = (-w_t.shape[0]) % BF16_ROWS
    w_t = jnp.pad(w_t, ((0, pad), (0, 0))).astype(BF16)
    assert w_cols.shape[1] == EVEN_ROPE_COLS + B_HEADS * B_VDIM and w_t.shape[0] >= EVEN_T_ROWS
    return w_cols, w_t


def kernel(x, p, norm_ffn_a, ffn_a_wg, ffn_a_wu, ffn_a_wd, norm_mix, norm_ffn_b, ffn_b_wg, ffn_b_wu,
           ffn_b_wd, norm_ple, ple_gate, ple_proj, even_w_in, even_w_out, diff_lambda_q1,
           diff_lambda_k1, diff_lambda_q2, diff_lambda_k2, diff_subln, odd_w_in, odd_w_out, final_norm):
    batch, seq, d_model = x.shape
    depth = p.shape[0]
    assert seq % ATTN_TQ == 0 and (batch * seq) % TOKEN_TILE == 0 and seq % TOKEN_TILE == 0
    assert seq == C_PATTERNS[-1][0], "dilated kernel assumes the widest window spans the sequence"
    top_k = min(A_TOPK_MAX, seq // 4)
    cos_t, sin_t = _rope_tables(seq)
    gain = lambda g: g.reshape(1, -1).astype(F32)
    n_buckets = seq // ATTN_TQ
    r3 = lambda a: a.reshape(batch, seq, a.shape[-1])
    p_all = p.reshape(depth, batch * seq, p.shape[-1])
    f32 = lambda w: w.astype(F32)
    ffn_a = (f32(ffn_a_wg), f32(ffn_a_wu), f32(ffn_a_wd))
    ffn_b = (f32(ffn_b_wg), f32(ffn_b_wu), f32(ffn_b_wd))

    h = x.reshape(batch * seq, d_model)
    for i in range(depth):
        ga, gm = gain(norm_ffn_a[i]), gain(norm_mix[i])
        if i % 2 == 0:
            e = i // 2
            lambda_init = 0.8 - 0.6 * math.exp(-0.3 * i)
            (ha, qa, qi, qb, kb, ka, ki, vb, va_t, wi_t) = _pre_even(
                h, ga, gm, *_pack_even_w_in(even_w_in[e]), cos_t, sin_t, *ffn_a, layer=i, batch=batch, seq=seq)
            ka, ki, vb = r3(ka), r3(ki), r3(vb)
            lam_vecs = jnp.stack([diff_lambda_q1[e], diff_lambda_k1[e],
                                  diff_lambda_q2[e], diff_lambda_k2[e]]).astype(F32)
            out_a = [_dsa_bucket(qa, qi, wi_t, ka, va_t, ki, j=j, top_k=top_k) for j in range(n_buckets)]
            out_b = _diff_attention(lam_vecs, gain(diff_subln[e]), qb, kb, vb, lambda_init=lambda_init)
            mixes = [jnp.concatenate(out_a, axis=2), out_b.reshape(batch * seq, -1)]
            mix_transposed = (True, False)
            w_out_all, mixer_layer = f32(even_w_out), e
        else:
            o = i // 2
            ha, q, k, v = _pre_odd(h, ga, gm, cos_t, sin_t, *ffn_a, f32(odd_w_in),
                                   layer=i, mixer_layer=o, seq=seq)
            mixes = [_dilated(r3(q), r3(k), r3(v)).reshape(batch * seq, -1)]
            mix_transposed = (False,)
            w_out_all, mixer_layer = f32(odd_w_out), o
        gf = gain(final_norm) if i == depth - 1 else None
        h = _post(ha, mixes, gain(norm_ffn_b[i]), gain(norm_ple[i]), p_all, gf,
                  w_out_all, *ffn_b, f32(ple_gate), f32(ple_proj),
                  mix_transposed=mix_transposed, layer=i, mixer_layer=mixer_layer, seq=seq)
    return h.reshape(batch, seq, d_model)
```

```python
import functools
import math

import jax
import jax.numpy as jnp
import numpy as np
from jax import lax
from jax.experimental import pallas as pl
from jax.experimental.pallas import tpu as pltpu

F32 = jnp.float32
BF16 = jnp.bfloat16
I32 = jnp.int32
I16 = jnp.int16

HEAD_DIM = 64
ROPE_THETA = 10000.0
NORM_EPS = 1e-6
SUBLN_EPS = 1e-5
NEG_INF = -1e30
A_HEADS = 8
A_TOPK_MAX = 256
IDX_HEADS = 8
B_HEADS = 4
B_VDIM = 2 * HEAD_DIM
C_PATTERNS = ((128, 1), (512, 4), (2048, 16))
DIL_SPAN = 128

LANES = 128
BF16_ROWS = 16
VMEM_LIMIT_BYTES = 56 * 1024 * 1024
TOKEN_TILE = 256
FFN_CHUNKS = 2
ATTN_TQ = 256
DIL_MAX_UNROLL = 16
WEIGHT_STAGE_ELEMS = 192 * 1024
WEIGHT_STAGE_SLOTS = 6

ATTN_Q_SCALE = (HEAD_DIM ** -0.5) * math.log2(math.e)

INT_MIN = -(2 ** 31)
KEY_NEG_INF = INT_MIN + 0x7FFFFF
KEY_POS_INF = 0x7F800000
IDX_BIG = 2 ** 30

EVEN_ROPE_COLS = 4 * 512 + 2 * HEAD_DIM
EVEN_T_ROWS = HEAD_DIM + IDX_HEADS

_CONTRACT_LAST = (((1,), (1,)), ((), ()))
_CONTRACT_FIRST = (((0,), (0,)), ((), ()))


def _cparams(n_grid):
    return pltpu.CompilerParams(dimension_semantics=("arbitrary",) * n_grid,
                                vmem_limit_bytes=VMEM_LIMIT_BYTES)


def _resident(shape):
    nd = len(shape)
    return pl.BlockSpec(shape, lambda *_: (0,) * nd, pipeline_mode=pl.Buffered(1))


_HBM = pl.BlockSpec(memory_space=pl.ANY)


def _stage_rows(rows, cols):
    return max(r for r in range(BF16_ROWS, rows + 1, BF16_ROWS)
               if rows % r == 0 and r * cols <= WEIGHT_STAGE_ELEMS)


def _weight_scratch(shapes):
    ns = WEIGHT_STAGE_SLOTS
    stage = (ns, max(_stage_rows(r, c) for r, c in shapes), max(c for _, c in shapes))
    return [pltpu.VMEM(s, BF16) for s in shapes] + [pltpu.VMEM(stage, F32), pltpu.SemaphoreType.DMA((ns,))]


def _fetch_weights(layer, hbm_refs, vmem_refs, stage, sem):
    @pl.when(pl.program_id(0) == 0)
    def _():
        for w_hbm, w_vmem in zip(hbm_refs, vmem_refs):
            rows, cols = w_vmem.shape
            rc = _stage_rows(rows, cols)
            n = rows // rc
            ns = stage.shape[0]
            ahead = min(ns - 1, n)

            def copy(c, w_hbm=w_hbm, rc=rc, cols=cols):
                slot = c % ns
                return pltpu.make_async_copy(w_hbm.at[layer, pl.ds(c * rc, rc), :],
                                             stage.at[slot, pl.ds(0, rc), pl.ds(0, cols)], sem.at[slot])

            def body(c, carry, copy=copy, w_vmem=w_vmem, rc=rc, cols=cols, n=n, ahead=ahead):
                @pl.when(c + ahead < n)
                def _():
                    copy(c + ahead).start()
                copy(c).wait()
                w_vmem[pl.ds(pl.multiple_of(c * rc, rc), rc), :] = stage[c % ns, 0:rc, 0:cols].astype(BF16)
                return carry

            for c in range(ahead):
                copy(c).start()
            lax.fori_loop(0, n, body, 0)


def _rms(x, g, eps):
    ms = jnp.mean(x * x, axis=-1, keepdims=True)
    return x * lax.rsqrt(ms + eps) * g


def _ffn_half(h, g, wg_ref, wu_ref, wd_ref):
    xn = _rms(h, g, NORM_EPS).astype(BF16)
    d_ff = wg_ref.shape[1]
    fc = d_ff // FFN_CHUNKS
    acc = None
    for c in range(FFN_CHUNKS):
        a = jnp.dot(xn, wg_ref[:, c * fc:(c + 1) * fc], preferred_element_type=F32)
        b = jnp.dot(xn, wu_ref[:, c * fc:(c + 1) * fc], preferred_element_type=F32)
        mid = (a / (1.0 + jnp.exp(-a)) * b).astype(BF16)
        part = jnp.dot(mid, wd_ref[c * fc:(c + 1) * fc, :], preferred_element_type=F32)
        acc = part if acc is None else acc + part
    return h + 0.5 * acc


def _rope(y, cos_t, sin_t):
    w = y.shape[1]
    n = w // LANES
    c = jnp.concatenate([cos_t] * n, axis=1) if n > 1 else cos_t
    s = jnp.concatenate([sin_t] * n, axis=1) if n > 1 else sin_t
    lane = lax.broadcasted_iota(I32, y.shape, 1)
    first_half = (lane & (HEAD_DIM // 2)) == 0
    partner = jnp.where(first_half,
                        pltpu.roll(y, w - HEAD_DIM // 2, 1),
                        pltpu.roll(y, HEAD_DIM // 2, 1))
    return y * c + partner * s


def _pre_even_kernel(h_ref, ga_ref, gm_ref, wr_ref, wt_ref, cos_ref, sin_ref, wg_hbm, wu_hbm, wd_hbm,
                     ha_ref, qa_ref, qi_ref, qb_ref, kb_ref, ka_ref, ki_ref, vb_ref, vat_ref, wit_ref,
                     wg_ref, wu_ref, wd_ref, stage, sem, *, layer, idx_scale):
    _fetch_weights(layer, (wg_hbm, wu_hbm, wd_hbm), (wg_ref, wu_ref, wd_ref), stage, sem)
    ha = _ffn_half(h_ref[...], ga_ref[...], wg_ref, wu_ref, wd_ref)
    ha_ref[...] = ha
    xn = _rms(ha, gm_ref[...], NORM_EPS).astype(BF16)
    y = jnp.dot(xn, wr_ref[...], preferred_element_type=F32)
    yr = _rope(y[:, :EVEN_ROPE_COLS], cos_ref[...], sin_ref[...])
    vb_ref[...] = y[:, EVEN_ROPE_COLS:EVEN_ROPE_COLS + B_HEADS * B_VDIM].astype(BF16)
    for hd in range(8):
        lo = hd * HEAD_DIM
        qa_ref[hd] = (yr[:, lo:lo + HEAD_DIM] * ATTN_Q_SCALE).astype(BF16)
        qi_ref[hd] = yr[:, 512 + lo:512 + lo + HEAD_DIM].astype(BF16)
        qb_ref[hd] = (yr[:, 1024 + lo:1024 + lo + HEAD_DIM] * ATTN_Q_SCALE).astype(BF16)
        kb_ref[hd] = yr[:, 1536 + lo:1536 + lo + HEAD_DIM].astype(BF16)
    ka_ref[...] = yr[:, 2048:2112].astype(BF16)
    ki_ref[...] = yr[:, 2112:2176].astype(BF16)
    yt = lax.dot_general(wt_ref[...], xn, _CONTRACT_LAST, preferred_element_type=F32)
    vat_ref[...] = yt[0:HEAD_DIM].astype(BF16)
    wit_ref[...] = yt[HEAD_DIM:HEAD_DIM + IDX_HEADS] * idx_scale


def _pre_odd_kernel(h_ref, ga_ref, gm_ref, cos_ref, sin_ref, wg_hbm, wu_hbm, wd_hbm, win_hbm,
                    ha_ref, q_ref, k_ref, v_ref,
                    wg_ref, wu_ref, wd_ref, win_ref, stage, sem, *, layer, mixer_layer):
    _fetch_weights(layer, (wg_hbm, wu_hbm, wd_hbm), (wg_ref, wu_ref, wd_ref), stage, sem)
    _fetch_weights(mixer_layer, (win_hbm,), (win_ref,), stage, sem)
    ha = _ffn_half(h_ref[...], ga_ref[...], wg_ref, wu_ref, wd_ref)
    ha_ref[...] = ha
    xn = _rms(ha, gm_ref[...], NORM_EPS).astype(BF16)
    y = jnp.dot(xn, win_ref[...], preferred_element_type=F32)
    d = q_ref.shape[1]
    yr = _rope(y[:, :2 * d], cos_ref[...], sin_ref[...])
    q_ref[...] = yr[:, :d] * ATTN_Q_SCALE
    k_ref[...] = yr[:, d:2 * d]
    v_ref[...] = y[:, 2 * d:3 * d]


def _token_specs(tm, seq_tiles):
    row = lambda w: pl.BlockSpec((tm, w), lambda i: (i, 0))
    rope = pl.BlockSpec((tm, LANES), lambda i: (i % seq_tiles, 0))
    return row, rope


def _pre_even(h, ga, gm, w_cols, w_t, cos_t, sin_t, wg_all, wu_all, wd_all, *, layer, batch, seq):
    m, d = h.shape
    tm = TOKEN_TILE
    nt = seq // tm
    row, rope = _token_specs(tm, nt)
    hm = lambda: pl.BlockSpec((None, 8, tm, HEAD_DIM), lambda i: (i // nt, 0, i % nt, 0))
    hm_shape = jax.ShapeDtypeStruct((batch, 8, seq, HEAD_DIM), BF16)
    tr = lambda rows: pl.BlockSpec((None, rows, tm), lambda i: (i // nt, 0, i % nt))
    idx_scale = (HEAD_DIM ** -0.5) * (IDX_HEADS ** -0.5)
    return pl.pallas_call(
        functools.partial(_pre_even_kernel, layer=layer, idx_scale=idx_scale),
        grid=(m // tm,),
        in_specs=[row(d), _resident(ga.shape), _resident(gm.shape), _resident(w_cols.shape),
                  _resident(w_t.shape), rope, rope, _HBM, _HBM, _HBM],
        out_specs=[row(d), hm(), hm(), hm(), hm(), row(HEAD_DIM), row(HEAD_DIM),
                   row(B_HEADS * B_VDIM), tr(HEAD_DIM), tr(IDX_HEADS)],
        out_shape=[jax.ShapeDtypeStruct((m, d), F32), hm_shape, hm_shape, hm_shape, hm_shape,
                   jax.ShapeDtypeStruct((m, HEAD_DIM), BF16), jax.ShapeDtypeStruct((m, HEAD_DIM), BF16),
                   jax.ShapeDtypeStruct((m, B_HEADS * B_VDIM), BF16),
                   jax.ShapeDtypeStruct((batch, HEAD_DIM, seq), BF16),
                   jax.ShapeDtypeStruct((batch, IDX_HEADS, seq), F32)],
        scratch_shapes=_weight_scratch([wg_all.shape[1:], wu_all.shape[1:], wd_all.shape[1:]]),
        compiler_params=_cparams(1),
        name="pre_even",
    )(h, ga, gm, w_cols, w_t, cos_t, sin_t, wg_all, wu_all, wd_all)


def _pre_odd(h, ga, gm, cos_t, sin_t, wg_all, wu_all, wd_all, win_all, *, layer, mixer_layer, seq):
    m, d = h.shape
    tm = TOKEN_TILE
    nt = seq // tm
    row, rope = _token_specs(tm, nt)
    out = jax.ShapeDtypeStruct((m, d), F32)
    return pl.pallas_call(
        functools.partial(_pre_odd_kernel, layer=layer, mixer_layer=mixer_layer),
        grid=(m // tm,),
        in_specs=[row(d), _resident(ga.shape), _resident(gm.shape), rope, rope, _HBM, _HBM, _HBM, _HBM],
        out_specs=[row(d), row(d), row(d), row(d)],
        out_shape=[out, out, out, out],
        scratch_shapes=_weight_scratch([wg_all.shape[1:], wu_all.shape[1:], wd_all.shape[1:], win_all.shape[1:]]),
        compiler_params=_cparams(1),
        name="pre_odd",
    )(h, ga, gm, cos_t, sin_t, wg_all, wu_all, wd_all, win_all)


def _post_kernel(*refs, n_mix, mix_transposed, final, layer, mixer_layer):
    refs = list(refs)
    h_ref = refs.pop(0)
    mix_refs = [refs.pop(0) for _ in range(n_mix)]
    gb_ref, gp_ref, p_ref = (refs.pop(0) for _ in range(3))
    gf_ref = refs.pop(0) if final else None
    wout_hbm, wg_hbm, wu_hbm, wd_hbm, pg_hbm, pp_hbm = (refs.pop(0) for _ in range(6))
    o_ref = refs.pop(0)
    wout_ref, wg_ref, wu_ref, wd_ref, pg_ref, pp_ref, stage, sem = refs
    _fetch_weights(mixer_layer, (wout_hbm,), (wout_ref,), stage, sem)
    _fetch_weights(layer, (wg_hbm, wu_hbm, wd_hbm, pg_hbm, pp_hbm), (wg_ref, wu_ref, wd_ref, pg_ref, pp_ref),
                   stage, sem)
    h1 = h_ref[...]
    row0 = 0
    for r, transposed in zip(mix_refs, mix_transposed):
        if transposed:
            w = wout_ref[row0:row0 + r.shape[0], :]
            h1 = h1 + lax.dot_general(r[...], w, _CONTRACT_FIRST, preferred_element_type=F32)
            row0 += r.shape[0]
        else:
            w = wout_ref[row0:row0 + r.shape[1], :]
            h1 = h1 + jnp.dot(r[...], w, preferred_element_type=F32)
            row0 += r.shape[1]
    h2 = _ffn_half(h1, gb_ref[...], wg_ref, wu_ref, wd_ref)
    z = jnp.dot(_rms(h2, gp_ref[...], NORM_EPS).astype(BF16), pg_ref[...], preferred_element_type=F32)
    gate = 1.0 / (1.0 + jnp.exp(-z))
    emb = jnp.dot(p_ref[...].astype(BF16), pp_ref[...], preferred_element_type=F32)
    h3 = h2 + gate * emb
    if final:
        h3 = _rms(h3, gf_ref[...], NORM_EPS)
    o_ref[...] = h3


def _post(h, mixes, gb, gp, p_all, gf, wout_all, wg_all, wu_all, wd_all, pg_all, pp_all,
          *, mix_transposed, layer, mixer_layer, seq):
    m, d = h.shape
    tm = TOKEN_TILE
    nt = seq // tm
    row = lambda w: pl.BlockSpec((tm, w), lambda i: (i, 0))
    mix_specs = [pl.BlockSpec((None, x.shape[1], tm), lambda i: (i // nt, 0, i % nt)) if t else row(x.shape[1])
                 for x, t in zip(mixes, mix_transposed)]
    final = gf is not None
    weights = [wout_all, wg_all, wu_all, wd_all, pg_all, pp_all]
    args = [h, *mixes, gb, gp, p_all] + ([gf] if final else []) + weights
    specs = ([row(d), *mix_specs, _resident(gb.shape), _resident(gp.shape),
              pl.BlockSpec((None, tm, p_all.shape[2]), lambda i: (layer, i, 0))]
             + ([_resident(gf.shape)] if final else []) + [_HBM] * len(weights))
    return pl.pallas_call(
        functools.partial(_post_kernel, n_mix=len(mixes), mix_transposed=mix_transposed, final=final,
                          layer=layer, mixer_layer=mixer_layer),
        grid=(m // tm,),
        in_specs=specs,
        out_specs=row(d),
        out_shape=jax.ShapeDtypeStruct((m, d), F32),
        scratch_shapes=_weight_scratch([w.shape[1:] for w in weights]),
        compiler_params=_cparams(1),
        name="post",
    )(*args)


def _key_count(mask):
    return jnp.sum(jnp.where(mask, 1.0, 0.0), axis=0, keepdims=True)


def _key_to_float(key):
    key = jnp.clip(key, jnp.int32(KEY_NEG_INF), jnp.int32(KEY_POS_INF))
    return pltpu.bitcast(jnp.where(key < 0, key ^ jnp.int32(0x7FFFFFFF), key), F32)


def _count16(mask):
    s = mask.shape[0]
    ones = jnp.where(mask, jnp.ones((), I16), jnp.zeros((), I16))
    parts = [ones[i:i + BF16_ROWS] for i in range(0, s, BF16_ROWS)]
    while len(parts) > 1:
        nxt = [parts[i] + parts[i + 1] for i in range(0, len(parts) - 1, 2)]
        parts = nxt + ([parts[-1]] if len(parts) % 2 else [])
    return jnp.sum(parts[0].astype(F32), axis=0, keepdims=True)


def _kth_largest_fast(val_scr, hi_scr, lo_scr, *, top_k):
    tq = val_scr.shape[1]
    half_min = jnp.int32(-(2 ** 15))

    def keys():
        bits = pltpu.bitcast(val_scr[...], I32)
        return jnp.where(bits < 0, bits ^ jnp.int32(0x7FFFFFFF), bits)

    hi_scr[...] = lax.shift_right_arithmetic(keys(), jnp.int32(16)).astype(I16)

    def search(ref, need):
        def step(it, acc):
            cand = acc + lax.shift_left(jnp.int32(1), jnp.int32(15) - it)
            cnt = _count16(ref[...] >= cand.astype(I16))
            return jnp.where(cnt >= need, cand, acc)
        return lax.fori_loop(0, 16, step, jnp.full((1, tq), half_min, I32))

    hi = search(hi_scr, top_k)
    hi16 = hi.astype(I16)
    need = top_k - _count16(hi_scr[...] > hi16)
    low = ((keys() & jnp.int32(0xFFFF)) + half_min).astype(I16)
    lo_scr[...] = jnp.where(hi_scr[...] == hi16, low, half_min.astype(I16))
    lo = search(lo_scr, need)
    return _key_to_float(lax.shift_left(hi, jnp.int32(16)) | ((lo - half_min) & jnp.int32(0xFFFF)))


def _kth_largest_float(val_scr, *, top_k):
    tq = val_scr.shape[1]

    def value_step(it, lo):
        cand = lo ^ lax.shift_left(jnp.int32(1), jnp.int32(31) - it)
        cnt = _key_count(val_scr[...] >= _key_to_float(cand))
        return jnp.where(cnt >= top_k, cand, lo)

    return _key_to_float(lax.fori_loop(0, 32, value_step, jnp.full((1, tq), INT_MIN, I32)))


def _topk_bias(score, causal, val_scr, idx_scr, hi_scr, lo_scr, thr_scr, *, top_k):
    s, tq = score.shape
    key_idx = lax.broadcasted_iota(I32, (s, tq), 0)
    val_scr[...] = jnp.where(causal, score, -jnp.inf)

    thr_scr[...] = jnp.broadcast_to(_kth_largest_fast(val_scr, hi_scr, lo_scr, top_k=top_k), thr_scr.shape)
    thr0 = thr_scr[0:1, :]
    verified = jnp.logical_and(jnp.min(_key_count(val_scr[...] >= thr0)) >= top_k,
                               jnp.max(_key_count(val_scr[...] > thr0)) < top_k)

    @pl.when(jnp.logical_not(verified))
    def _():
        thr_scr[...] = jnp.broadcast_to(_kth_largest_float(val_scr, top_k=top_k), thr_scr.shape)

    thr = thr_scr[0:1, :]
    n_ge = _key_count(val_scr[...] >= thr)
    excess = jnp.max(n_ge) > top_k

    @pl.when(jnp.logical_not(excess))
    def _():
        val_scr[...] = jnp.where(val_scr[...] >= thr, 0.0, NEG_INF)

    @pl.when(excess)
    def _():
        vals = val_scr[...]
        above = vals > thr
        need = top_k - _key_count(above)
        idx_scr[...] = jnp.where(vals == thr, key_idx, jnp.int32(IDX_BIG))
        n_bits = max(1, int(math.ceil(math.log2(s))))

        def index_step(it, p):
            cand = p + lax.shift_left(jnp.int32(1), jnp.int32(n_bits - 1) - it)
            cnt = _key_count(idx_scr[...] < cand)
            return jnp.where(cnt < need, cand, p)

        p = lax.fori_loop(0, n_bits, index_step, jnp.zeros((1, tq), I32))
        keep_tie = idx_scr[...] <= p
        keep = jnp.where(above, 0.0, jnp.where(keep_tie, 0.0, NEG_INF))
        val_scr[...] = jnp.where(causal, keep, NEG_INF)

    return val_scr[...]


def _with_ones_rows(v_t):
    return jnp.concatenate([v_t, jnp.ones((BF16_ROWS, v_t.shape[1]), BF16)], axis=0)


def _dsa_kernel(qa_ref, qi_ref, wit_ref, ka_ref, vat_ref, ki_ref, o_ref, *scratch, s_len, q_start, top_k):
    tq = o_ref.shape[1]
    key_idx = lax.broadcasted_iota(I32, (s_len, tq), 0)
    q_pos = lax.broadcasted_iota(I32, (s_len, tq), 1) + q_start
    causal = key_idx <= q_pos
    if s_len > top_k:
        ki = ki_ref[...]
        w_t = wit_ref[...]
        score = jnp.zeros((s_len, tq), F32)
        for hd in range(IDX_HEADS):
            dots = lax.dot_general(ki, qi_ref[hd], _CONTRACT_LAST, preferred_element_type=F32)
            score = score + w_t[hd:hd + 1, :] * jnp.maximum(dots, 0.0)
        bias = _topk_bias(score, causal, *scratch, top_k=top_k)
    else:
        bias = jnp.where(causal, 0.0, NEG_INF)
    ka = ka_ref[...]
    v_ext = _with_ones_rows(vat_ref[...])
    for hd in range(A_HEADS):
        s_t = lax.dot_general(ka, qa_ref[hd], _CONTRACT_LAST, preferred_element_type=F32) + bias
        m = jnp.max(s_t, axis=0, keepdims=True)
        p = jnp.exp2(s_t - m).astype(BF16)
        r = jnp.dot(v_ext, p, preferred_element_type=F32)
        o = r[0:HEAD_DIM] * (1.0 / r[HEAD_DIM:HEAD_DIM + 1])
        o_ref[hd * HEAD_DIM:(hd + 1) * HEAD_DIM, :] = o.astype(BF16)


def _dsa_bucket(qa, qi, wi_t, ka, va_t, ki, *, j, top_k):
    batch, _, seq, _ = qa.shape
    tq = ATTN_TQ
    s_len = (j + 1) * tq
    hm = pl.BlockSpec((None, 8, tq, HEAD_DIM), lambda b: (b, 0, j, 0))
    kv = pl.BlockSpec((None, s_len, HEAD_DIM), lambda b: (b, 0, 0))
    scratch = [pltpu.VMEM((s_len, tq), F32), pltpu.VMEM((s_len, tq), I32), pltpu.VMEM((s_len, tq), I16),
               pltpu.VMEM((s_len, tq), I16), pltpu.VMEM((8, tq), F32)] if s_len > top_k else []
    return pl.pallas_call(
        functools.partial(_dsa_kernel, s_len=s_len, q_start=j * tq, top_k=top_k),
        grid=(batch,),
        in_specs=[hm, hm, pl.BlockSpec((None, IDX_HEADS, tq), lambda b: (b, 0, j)), kv,
                  pl.BlockSpec((None, HEAD_DIM, s_len), lambda b: (b, 0, 0)), kv],
        out_specs=pl.BlockSpec((None, A_HEADS * HEAD_DIM, tq), lambda b: (b, 0, 0)),
        out_shape=jax.ShapeDtypeStruct((batch, A_HEADS * HEAD_DIM, tq), BF16),
        scratch_shapes=scratch,
        compiler_params=_cparams(1),
        name=f"dsa_{j}",
    )(qa, qi, wi_t, ka, va_t, ki)


def _causal_attend(q, k_ref, v_ext, c, j, tri_bias):
    tq = ATTN_TQ
    off = j * tq
    s_d = lax.dot_general(q, k_ref[c, off:off + tq, :], _CONTRACT_LAST, preferred_element_type=F32) + tri_bias
    m = jnp.max(s_d, axis=1, keepdims=True)
    if j > 0:
        s_o = lax.dot_general(q, k_ref[c, 0:off, :], _CONTRACT_LAST, preferred_element_type=F32)
        m = jnp.maximum(m, jnp.max(s_o, axis=1, keepdims=True))
    r = jnp.dot(jnp.exp2(s_d - m).astype(BF16), v_ext[off:off + tq, :], preferred_element_type=F32)
    if j > 0:
        r = r + jnp.dot(jnp.exp2(s_o - m).astype(BF16), v_ext[0:off, :], preferred_element_type=F32)
    return r[:, :B_VDIM] * (1.0 / r[:, B_VDIM:])


def _diff_kernel(lam_ref, subln_ref, q_ref, k_ref, v_ref, o_ref, *, lambda_init):
    seq = o_ref.shape[0]
    tq = ATTN_TQ
    lv = lam_ref[...]
    lam = (jnp.exp(jnp.sum(lv[0:1] * lv[1:2], axis=1, keepdims=True))
           - jnp.exp(jnp.sum(lv[2:3] * lv[3:4], axis=1, keepdims=True)) + lambda_init)
    row = lax.broadcasted_iota(I32, (tq, tq), 0)
    col = lax.broadcasted_iota(I32, (tq, tq), 1)
    tri_bias = jnp.where(col <= row, 0.0, NEG_INF)
    subln = subln_ref[...]
    v = v_ref[...]
    v_ext = jnp.concatenate([v, jnp.ones(v.shape, BF16)], axis=1)
    for j in range(seq // tq):
        rows = slice(j * tq, (j + 1) * tq)
        a = (_causal_attend(q_ref[0, rows, :], k_ref, v_ext, 0, j, tri_bias)
             - lam * _causal_attend(q_ref[1, rows, :], k_ref, v_ext, 1, j, tri_bias))
        y = _rms(a, subln, SUBLN_EPS) * (1.0 - lambda_init)
        o_ref[rows, :] = y.astype(BF16)


def _diff_attention(lam_vecs, subln, qb, kb, vb, *, lambda_init):
    batch, _, seq, _ = qb.shape
    qk = lambda: pl.BlockSpec((None, 2, seq, HEAD_DIM), lambda b, h: (b, h, 0, 0))
    vo = lambda: pl.BlockSpec((None, seq, B_VDIM), lambda b, h: (b, 0, h))
    return pl.pallas_call(
        functools.partial(_diff_kernel, lambda_init=lambda_init),
        grid=(batch, B_HEADS),
        in_specs=[pl.BlockSpec(lam_vecs.shape, lambda b, h: (0, 0)),
                  pl.BlockSpec(subln.shape, lambda b, h: (0, 0)),
                  qk(), qk(), vo()],
        out_specs=vo(),
        out_shape=jax.ShapeDtypeStruct((batch, seq, B_HEADS * B_VDIM), BF16),
        compiler_params=_cparams(2),
        name="diff",
    )(lam_vecs, subln, qb, kb, vb)


def _unroll_for(trips):
    return max(u for u in range(1, DIL_MAX_UNROLL + 1) if trips % u == 0)


def _dil_kernel(q_ref, k_ref, v_ref, o_ref, m_scr, l_scr, acc_scr, bias_scr):
    seq = q_ref.shape[0]
    span = DIL_SPAN
    lane = lax.broadcasted_iota(I32, (span, LANES), 1)
    head0 = lane < HEAD_DIM

    qi = lax.broadcasted_iota(I32, (2 * span, 2 * span), 0) & (span - 1)
    kj = lax.broadcasted_iota(I32, (2 * span, 2 * span), 1)
    dist = kj - qi
    bias_scr[...] = jnp.where(jnp.where(kj < span, dist, span - dist) >= 0, 0.0, NEG_INF)

    def rows_at(start, d):
        return pl.ds(start, span) if d == 1 else pl.ds(start, span, stride=d)

    def tile(t, *, d, blk0, first):
        r = t % d
        blk = blk0 + t // d
        start = r + blk * (span * d)
        if d == 1:
            start = pl.multiple_of(start, span)
        rows = rows_at(start, d)
        qb = q_ref[rows, :]
        qs = jnp.concatenate([jnp.where(head0, qb, 0.0), jnp.where(head0, 0.0, qb)], axis=0).astype(BF16)
        kk = k_ref[rows, :]
        vv = v_ref[rows, :]
        has_prev = blk0 > 0
        if has_prev:
            prev = rows_at(start - span * d, d)
            kk = jnp.concatenate([k_ref[prev, :], kk], axis=0)
            vv = jnp.concatenate([v_ref[prev, :], vv], axis=0)
            bias = bias_scr[...]
        else:
            bias = bias_scr[:, span:]
        s = lax.dot_general(qs, kk.astype(BF16), _CONTRACT_LAST, preferred_element_type=F32) + bias
        m = jnp.max(s, axis=1, keepdims=True)
        p = jnp.exp2(s - m).astype(BF16)
        v_ext = jnp.concatenate([vv.astype(BF16), jnp.ones(vv.shape, BF16)], axis=1)
        pv = jnp.dot(p, v_ext, preferred_element_type=F32)
        acc_t = jnp.where(head0, pv[:span, :LANES], pv[span:, :LANES])
        l_t = jnp.where(head0, pv[:span, LANES:], pv[span:, LANES:])
        m_t = jnp.where(head0, m[:span], m[span:])
        if first:
            m_scr[rows, :] = m_t
            l_scr[rows, :] = l_t
            acc_scr[rows, :] = acc_t
        else:
            m_o = m_scr[rows, :]
            m_n = jnp.maximum(m_o, m_t)
            ea = jnp.exp2(m_o - m_n)
            eb = jnp.exp2(m_t - m_n)
            l_scr[rows, :] = ea * l_scr[rows, :] + eb * l_t
            acc_scr[rows, :] = ea * acc_scr[rows, :] + eb * acc_t
            m_scr[rows, :] = m_n

    for pi, (window, d) in enumerate(reversed(C_PATTERNS)):
        n_blocks = (seq // d) // span
        first = pi == 0

        def no_prev(t, carry, d=d, first=first):
            tile(t, d=d, blk0=0, first=first)
            return carry

        def with_prev(t, carry, d=d, first=first):
            tile(t, d=d, blk0=1, first=first)
            return carry

        lax.fori_loop(0, d, no_prev, 0, unroll=_unroll_for(d))
        if n_blocks > 1:
            trips = d * (n_blocks - 1)
            lax.fori_loop(0, trips, with_prev, 0, unroll=_unroll_for(trips))

    o_ref[...] = (acc_scr[...] * (1.0 / l_scr[...])).astype(BF16)


def _dilated(q, k, v):
    batch, seq, d = q.shape
    n_pairs = d // LANES
    blk = lambda: pl.BlockSpec((None, seq, LANES), lambda b, hp: (b, 0, hp))
    return pl.pallas_call(
        _dil_kernel,
        grid=(batch, n_pairs),
        in_specs=[blk(), blk(), blk()],
        out_specs=blk(),
        out_shape=jax.ShapeDtypeStruct((batch, seq, d), BF16),
        scratch_shapes=[pltpu.VMEM((seq, LANES), F32)] * 3 + [pltpu.VMEM((2 * DIL_SPAN, 2 * DIL_SPAN), F32)],
        compiler_params=_cparams(2),
        name="dilated",
    )(q, k, v)


def _rope_tables(seq):
    inv = 1.0 / (ROPE_THETA ** (jnp.arange(0, HEAD_DIM, 2, dtype=F32) / HEAD_DIM))
    ang = jnp.arange(seq, dtype=F32)[:, None] * inv[None, :]
    cos, sin = jnp.cos(ang), jnp.sin(ang)
    cos_t = jnp.concatenate([cos, cos, cos, cos], axis=1)
    sin_t = jnp.concatenate([-sin, sin, -sin, sin], axis=1)
    return cos_t, sin_t


def _pack_even_w_in(w):
    cuts = np.cumsum([0, 512, 64, 64, 512, 64, 8, 512, 512, 512])
    qa, ka, va, qi, ki, wi, qb, kb, vb = [w[:, cuts[i]:cuts[i + 1]] for i in range(9)]
    w_cols = jnp.concatenate([qa, qi, qb, kb, ka, ki, vb], axis=1).astype(BF16)
    w_t = jnp.concatenate([va, wi], axis=1).T
    pad = (-w_t.shape[0]) % BF16_ROWS
    w_t = jnp.pad(w_t, ((0, pad), (0, 0))).astype(BF16)
    assert w_cols.shape[1] == EVEN_ROPE_COLS + B_HEADS * B_VDIM and w_t.shape[0] >= EVEN_T_ROWS
    return w_cols, w_t


def kernel(x, p, norm_ffn_a, ffn_a_wg, ffn_a_wu, ffn_a_wd, norm_mix, norm_ffn_b, ffn_b_wg, ffn_b_wu,
           ffn_b_wd, norm_ple, ple_gate, ple_proj, even_w_in, even_w_out, diff_lambda_q1,
           diff_lambda_k1, diff_lambda_q2, diff_lambda_k2, diff_subln, odd_w_in, odd_w_out, final_norm):
    batch, seq, d_model = x.shape
    depth = p.shape[0]
    assert seq % ATTN_TQ == 0 and (batch * seq) % TOKEN_TILE == 0 and seq % TOKEN_TILE == 0
    assert seq == C_PATTERNS[-1][0], "dilated kernel assumes the widest window spans the sequence"
    top_k = min(A_TOPK_MAX, seq // 4)
    cos_t, sin_t = _rope_tables(seq)
    gain = lambda g: g.reshape(1, -1).astype(F32)
    n_buckets = seq // ATTN_TQ
    r3 = lambda a: a.reshape(batch, seq, a.shape[-1])
    p_all = p.reshape(depth, batch * seq, p.shape[-1])
    f32 = lambda w: w.astype(F32)
    ffn_a = (f32(ffn_a_wg), f32(ffn_a_wu), f32(ffn_a_wd))
    ffn_b = (f32(ffn_b_wg), f32(ffn_b_wu), f32(ffn_b_wd))

    h = x.reshape(batch * seq, d_model)
    for i in range(depth):
        ga, gm = gain(norm_ffn_a[i]), gain(norm_mix[i])
        if i % 2 == 0:
            e = i // 2
            lambda_init = 0.8 - 0.6 * math.exp(-0.3 * i)
            (ha, qa, qi, qb, kb, ka, ki, vb, va_t, wi_t) = _pre_even(
                h, ga, gm, *_pack_even_w_in(even_w_in[e]), cos_t, sin_t, *ffn_a, layer=i, batch=batch, seq=seq)
            ka, ki, vb = r3(ka), r3(ki), r3(vb)
            lam_vecs = jnp.stack([diff_lambda_q1[e], diff_lambda_k1[e],
                                  diff_lambda_q2[e], diff_lambda_k2[e]]).astype(F32)
            out_a = [_dsa_bucket(qa, qi, wi_t, ka, va_t, ki, j=j, top_k=top_k) for j in range(n_buckets)]
            out_b = _diff_attention(lam_vecs, gain(diff_subln[e]), qb, kb, vb, lambda_init=lambda_init)
            mixes = [jnp.concatenate(out_a, axis=2), out_b.reshape(batch * seq, -1)]
            mix_transposed = (True, False)
            w_out_all, mixer_layer = f32(even_w_out), e
        else:
            o = i // 2
            ha, q, k, v = _pre_odd(h, ga, gm, cos_t, sin_t, *ffn_a, f32(odd_w_in),
                                   layer=i, mixer_layer=o, seq=seq)
            mixes = [_dilated(r3(q), r3(k), r3(v)).reshape(batch * seq, -1)]
            mix_transposed = (False,)
            w_out_all, mixer_layer = f32(odd_w_out), o
        gf = gain(final_norm) if i == depth - 1 else None
        h = _post(ha, mixes, gain(norm_ffn_b[i]), gain(norm_ple[i]), p_all, gf,
                  w_out_all, *ffn_b, f32(ple_gate), f32(ple_proj),
                  mix_transposed=mix_transposed, layer=i, mixer_layer=mixer_layer, seq=seq)
    return h.reshape(batch, seq, d_model)
```

```python
import functools
import math

import jax
import jax.numpy as jnp
import numpy as np
from jax import lax
from jax.experimental import pallas as pl
from jax.experimental.pallas import tpu as pltpu

F32 = jnp.float32
BF16 = jnp.bfloat16
I32 = jnp.int32
I16 = jnp.int16

HEAD_DIM = 64
ROPE_THETA = 10000.0
NORM_EPS = 1e-6
SUBLN_EPS = 1e-5
NEG_INF = -1e30
A_HEADS = 8
A_TOPK_MAX = 256
IDX_HEADS = 8
B_HEADS = 4
B_VDIM = 2 * HEAD_DIM
C_PATTERNS = ((128, 1), (512, 4), (2048, 16))
DIL_SPAN = 128

LANES = 128
BF16_ROWS = 16
VMEM_LIMIT_BYTES = 56 * 1024 * 1024
TOKEN_TILE = 256
FFN_CHUNKS = 2
ATTN_TQ = 256
DIL_MAX_UNROLL = 16
WEIGHT_STAGE_ELEMS = 192 * 1024
WEIGHT_STAGE_SLOTS = 6
DSA_HEAD_GROUP = 4

ATTN_Q_SCALE = (HEAD_DIM ** -0.5) * math.log2(math.e)

INT_MIN = -(2 ** 31)
KEY_NEG_INF = INT_MIN + 0x7FFFFF
KEY_POS_INF = 0x7F800000
IDX_BIG = 2 ** 30

EVEN_ROPE_COLS = 4 * 512 + 2 * HEAD_DIM
EVEN_T_ROWS = HEAD_DIM + IDX_HEADS

_CONTRACT_LAST = (((1,), (1,)), ((), ()))
_CONTRACT_FIRST = (((0,), (0,)), ((), ()))


def _cparams(n_grid):
    return pltpu.CompilerParams(dimension_semantics=("arbitrary",) * n_grid,
                                vmem_limit_bytes=VMEM_LIMIT_BYTES)


def _resident(shape):
    nd = len(shape)
    return pl.BlockSpec(shape, lambda *_: (0,) * nd, pipeline_mode=pl.Buffered(1))


_HBM = pl.BlockSpec(memory_space=pl.ANY)


def _stage_rows(rows, cols):
    return max(r for r in range(BF16_ROWS, rows + 1, BF16_ROWS)
               if rows % r == 0 and r * cols <= WEIGHT_STAGE_ELEMS)


def _weight_scratch(shapes):
    ns = WEIGHT_STAGE_SLOTS
    stage = (ns, max(_stage_rows(r, c) for r, c in shapes), max(c for _, c in shapes))
    return [pltpu.VMEM(s, BF16) for s in shapes] + [pltpu.VMEM(stage, F32), pltpu.SemaphoreType.DMA((ns,))]


def _fetch_weights(layer, hbm_refs, vmem_refs, stage, sem):
    @pl.when(pl.program_id(0) == 0)
    def _():
        for w_hbm, w_vmem in zip(hbm_refs, vmem_refs):
            rows, cols = w_vmem.shape
            rc = _stage_rows(rows, cols)
            n = rows // rc
            ns = stage.shape[0]
            ahead = min(ns - 1, n)

            def copy(c, w_hbm=w_hbm, rc=rc, cols=cols):
                slot = c % ns
                return pltpu.make_async_copy(w_hbm.at[layer, pl.ds(c * rc, rc), :],
                                             stage.at[slot, pl.ds(0, rc), pl.ds(0, cols)], sem.at[slot])

            def body(c, carry, copy=copy, w_vmem=w_vmem, rc=rc, cols=cols, n=n, ahead=ahead):
                @pl.when(c + ahead < n)
                def _():
                    copy(c + ahead).start()
                copy(c).wait()
                w_vmem[pl.ds(pl.multiple_of(c * rc, rc), rc), :] = stage[c % ns, 0:rc, 0:cols].astype(BF16)
                return carry

            for c in range(ahead):
                copy(c).start()
            lax.fori_loop(0, n, body, 0)


def _rms(x, g, eps):
    ms = jnp.mean(x * x, axis=-1, keepdims=True)
    return x * lax.rsqrt(ms + eps) * g


def _ffn_half(h, g, wg_ref, wu_ref, wd_ref):
    xn = _rms(h, g, NORM_EPS).astype(BF16)
    d_ff = wg_ref.shape[1]
    fc = d_ff // FFN_CHUNKS
    acc = None
    for c in range(FFN_CHUNKS):
        a = jnp.dot(xn, wg_ref[:, c * fc:(c + 1) * fc], preferred_element_type=F32)
        b = jnp.dot(xn, wu_ref[:, c * fc:(c + 1) * fc], preferred_element_type=F32)
        mid = (a / (1.0 + jnp.exp(-a)) * b).astype(BF16)
        part = jnp.dot(mid, wd_ref[c * fc:(c + 1) * fc, :], preferred_element_type=F32)
        acc = part if acc is None else acc + part
    return h + 0.5 * acc


def _rope(y, cos_t, sin_t):
    w = y.shape[1]
    n = w // LANES
    c = jnp.concatenate([cos_t] * n, axis=1) if n > 1 else cos_t
    s = jnp.concatenate([sin_t] * n, axis=1) if n > 1 else sin_t
    lane = lax.broadcasted_iota(I32, y.shape, 1)
    first_half = (lane & (HEAD_DIM // 2)) == 0
    partner = jnp.where(first_half,
                        pltpu.roll(y, w - HEAD_DIM // 2, 1),
                        pltpu.roll(y, HEAD_DIM // 2, 1))
    return y * c + partner * s


def _pre_even_kernel(h_ref, ga_ref, gm_ref, wr_ref, wt_ref, cos_ref, sin_ref, wg_hbm, wu_hbm, wd_hbm,
                     ha_ref, qa_ref, qi_ref, qb_ref, kb_ref, ka_ref, ki_ref, vb_ref, vat_ref, wit_ref,
                     wg_ref, wu_ref, wd_ref, stage, sem, *, layer, idx_scale):
    _fetch_weights(layer, (wg_hbm, wu_hbm, wd_hbm), (wg_ref, wu_ref, wd_ref), stage, sem)
    ha = _ffn_half(h_ref[...], ga_ref[...], wg_ref, wu_ref, wd_ref)
    ha_ref[...] = ha
    xn = _rms(ha, gm_ref[...], NORM_EPS).astype(BF16)
    y = jnp.dot(xn, wr_ref[...], preferred_element_type=F32)
    yr = _rope(y[:, :EVEN_ROPE_COLS], cos_ref[...], sin_ref[...])
    vb_ref[...] = y[:, EVEN_ROPE_COLS:EVEN_ROPE_COLS + B_HEADS * B_VDIM].astype(BF16)
    for hd in range(8):
        lo = hd * HEAD_DIM
        qa_ref[hd] = (yr[:, lo:lo + HEAD_DIM] * ATTN_Q_SCALE).astype(BF16)
        qi_ref[hd] = yr[:, 512 + lo:512 + lo + HEAD_DIM].astype(BF16)
        qb_ref[hd] = (yr[:, 1024 + lo:1024 + lo + HEAD_DIM] * ATTN_Q_SCALE).astype(BF16)
        kb_ref[hd] = yr[:, 1536 + lo:1536 + lo + HEAD_DIM].astype(BF16)
    ka_ref[...] = yr[:, 2048:2112].astype(BF16)
    ki_ref[...] = yr[:, 2112:2176].astype(BF16)
    yt = lax.dot_general(wt_ref[...], xn, _CONTRACT_LAST, preferred_element_type=F32)
    vat_ref[...] = yt[0:HEAD_DIM].astype(BF16)
    wit_ref[...] = yt[HEAD_DIM:HEAD_DIM + IDX_HEADS] * idx_scale


def _pre_odd_kernel(h_ref, ga_ref, gm_ref, cos_ref, sin_ref, wg_hbm, wu_hbm, wd_hbm, win_hbm,
                    ha_ref, q_ref, k_ref, v_ref,
                    wg_ref, wu_ref, wd_ref, win_ref, stage, sem, *, layer, mixer_layer):
    _fetch_weights(layer, (wg_hbm, wu_hbm, wd_hbm), (wg_ref, wu_ref, wd_ref), stage, sem)
    _fetch_weights(mixer_layer, (win_hbm,), (win_ref,), stage, sem)
    ha = _ffn_half(h_ref[...], ga_ref[...], wg_ref, wu_ref, wd_ref)
    ha_ref[...] = ha
    xn = _rms(ha, gm_ref[...], NORM_EPS).astype(BF16)
    y = jnp.dot(xn, win_ref[...], preferred_element_type=F32)
    d = q_ref.shape[1]
    yr = _rope(y[:, :2 * d], cos_ref[...], sin_ref[...])
    q_ref[...] = yr[:, :d] * ATTN_Q_SCALE
    k_ref[...] = yr[:, d:2 * d]
    v_ref[...] = y[:, 2 * d:3 * d]


def _token_specs(tm, seq_tiles):
    row = lambda w: pl.BlockSpec((tm, w), lambda i: (i, 0))
    rope = pl.BlockSpec((tm, LANES), lambda i: (i % seq_tiles, 0))
    return row, rope


def _pre_even(h, ga, gm, w_cols, w_t, cos_t, sin_t, wg_all, wu_all, wd_all, *, layer, batch, seq):
    m, d = h.shape
    tm = TOKEN_TILE
    nt = seq // tm
    row, rope = _token_specs(tm, nt)
    hm = lambda: pl.BlockSpec((None, 8, tm, HEAD_DIM), lambda i: (i // nt, 0, i % nt, 0))
    hm_shape = jax.ShapeDtypeStruct((batch, 8, seq, HEAD_DIM), BF16)
    tr = lambda rows: pl.BlockSpec((None, rows, tm), lambda i: (i // nt, 0, i % nt))
    idx_scale = (HEAD_DIM ** -0.5) * (IDX_HEADS ** -0.5)
    return pl.pallas_call(
        functools.partial(_pre_even_kernel, layer=layer, idx_scale=idx_scale),
        grid=(m // tm,),
        in_specs=[row(d), _resident(ga.shape), _resident(gm.shape), _resident(w_cols.shape),
                  _resident(w_t.shape), rope, rope, _HBM, _HBM, _HBM],
        out_specs=[row(d), hm(), hm(), hm(), hm(), row(HEAD_DIM), row(HEAD_DIM),
                   row(B_HEADS * B_VDIM), tr(HEAD_DIM), tr(IDX_HEADS)],
        out_shape=[jax.ShapeDtypeStruct((m, d), F32), hm_shape, hm_shape, hm_shape, hm_shape,
                   jax.ShapeDtypeStruct((m, HEAD_DIM), BF16), jax.ShapeDtypeStruct((m, HEAD_DIM), BF16),
                   jax.ShapeDtypeStruct((m, B_HEADS * B_VDIM), BF16),
                   jax.ShapeDtypeStruct((batch, HEAD_DIM, seq), BF16),
                   jax.ShapeDtypeStruct((batch, IDX_HEADS, seq), F32)],
        scratch_shapes=_weight_scratch([wg_all.shape[1:], wu_all.shape[1:], wd_all.shape[1:]]),
        compiler_params=_cparams(1),
        name="pre_even",
    )(h, ga, gm, w_cols, w_t, cos_t, sin_t, wg_all, wu_all, wd_all)


def _pre_odd(h, ga, gm, cos_t, sin_t, wg_all, wu_all, wd_all, win_all, *, layer, mixer_layer, seq):
    m, d = h.shape
    tm = TOKEN_TILE
    nt = seq // tm
    row, rope = _token_specs(tm, nt)
    out = jax.ShapeDtypeStruct((m, d), F32)
    return pl.pallas_call(
        functools.partial(_pre_odd_kernel, layer=layer, mixer_layer=mixer_layer),
        grid=(m // tm,),
        in_specs=[row(d), _resident(ga.shape), _resident(gm.shape), rope, rope, _HBM, _HBM, _HBM, _HBM],
        out_specs=[row(d), row(d), row(d), row(d)],
        out_shape=[out, out, out, out],
        scratch_shapes=_weight_scratch([wg_all.shape[1:], wu_all.shape[1:], wd_all.shape[1:], win_all.shape[1:]]),
        compiler_params=_cparams(1),
        name="pre_odd",
    )(h, ga, gm, cos_t, sin_t, wg_all, wu_all, wd_all, win_all)


def _post_kernel(*refs, n_mix, mix_transposed, final, layer, mixer_layer):
    refs = list(refs)
    h_ref = refs.pop(0)
    mix_refs = [refs.pop(0) for _ in range(n_mix)]
    gb_ref, gp_ref, p_ref = (refs.pop(0) for _ in range(3))
    gf_ref = refs.pop(0) if final else None
    wout_hbm, wg_hbm, wu_hbm, wd_hbm, pg_hbm, pp_hbm = (refs.pop(0) for _ in range(6))
    o_ref = refs.pop(0)
    wout_ref, wg_ref, wu_ref, wd_ref, pg_ref, pp_ref, stage, sem = refs
    _fetch_weights(mixer_layer, (wout_hbm,), (wout_ref,), stage, sem)
    _fetch_weights(layer, (wg_hbm, wu_hbm, wd_hbm, pg_hbm, pp_hbm), (wg_ref, wu_ref, wd_ref, pg_ref, pp_ref),
                   stage, sem)
    h1 = h_ref[...]
    row0 = 0
    for r, transposed in zip(mix_refs, mix_transposed):
        if transposed:
            w = wout_ref[row0:row0 + r.shape[0], :]
            h1 = h1 + lax.dot_general(r[...], w, _CONTRACT_FIRST, preferred_element_type=F32)
            row0 += r.shape[0]
        else:
            w = wout_ref[row0:row0 + r.shape[1], :]
            h1 = h1 + jnp.dot(r[...], w, preferred_element_type=F32)
            row0 += r.shape[1]
    h2 = _ffn_half(h1, gb_ref[...], wg_ref, wu_ref, wd_ref)
    z = jnp.dot(_rms(h2, gp_ref[...], NORM_EPS).astype(BF16), pg_ref[...], preferred_element_type=F32)
    gate = 1.0 / (1.0 + jnp.exp(-z))
    emb = jnp.dot(p_ref[...].astype(BF16), pp_ref[...], preferred_element_type=F32)
    h3 = h2 + gate * emb
    if final:
        h3 = _rms(h3, gf_ref[...], NORM_EPS)
    o_ref[...] = h3


def _post(h, mixes, gb, gp, p_all, gf, wout_all, wg_all, wu_all, wd_all, pg_all, pp_all,
          *, mix_transposed, layer, mixer_layer, seq):
    m, d = h.shape
    tm = TOKEN_TILE
    nt = seq // tm
    row = lambda w: pl.BlockSpec((tm, w), lambda i: (i, 0))
    mix_specs = [pl.BlockSpec((None, x.shape[1], tm), lambda i: (i // nt, 0, i % nt)) if t else row(x.shape[1])
                 for x, t in zip(mixes, mix_transposed)]
    final = gf is not None
    weights = [wout_all, wg_all, wu_all, wd_all, pg_all, pp_all]
    args = [h, *mixes, gb, gp, p_all] + ([gf] if final else []) + weights
    specs = ([row(d), *mix_specs, _resident(gb.shape), _resident(gp.shape),
              pl.BlockSpec((None, tm, p_all.shape[2]), lambda i: (layer, i, 0))]
             + ([_resident(gf.shape)] if final else []) + [_HBM] * len(weights))
    return pl.pallas_call(
        functools.partial(_post_kernel, n_mix=len(mixes), mix_transposed=mix_transposed, final=final,
                          layer=layer, mixer_layer=mixer_layer),
        grid=(m // tm,),
        in_specs=specs,
        out_specs=row(d),
        out_shape=jax.ShapeDtypeStruct((m, d), F32),
        scratch_shapes=_weight_scratch([w.shape[1:] for w in weights]),
        compiler_params=_cparams(1),
        name="post",
    )(*args)


def _key_count(mask):
    return jnp.sum(jnp.where(mask, 1.0, 0.0), axis=0, keepdims=True)


def _key_to_float(key):
    key = jnp.clip(key, jnp.int32(KEY_NEG_INF), jnp.int32(KEY_POS_INF))
    return pltpu.bitcast(jnp.where(key < 0, key ^ jnp.int32(0x7FFFFFFF), key), F32)


def _count16(mask):
    s = mask.shape[0]
    ones = jnp.where(mask, jnp.ones((), I16), jnp.zeros((), I16))
    parts = [ones[i:i + BF16_ROWS] for i in range(0, s, BF16_ROWS)]
    while len(parts) > 1:
        nxt = [parts[i] + parts[i + 1] for i in range(0, len(parts) - 1, 2)]
        parts = nxt + ([parts[-1]] if len(parts) % 2 else [])
    return jnp.sum(parts[0].astype(F32), axis=0, keepdims=True)


def _kth_largest_fast(val_scr, hi_scr, lo_scr, *, top_k):
    tq = val_scr.shape[1]
    half_min = jnp.int32(-(2 ** 15))

    def keys():
        bits = pltpu.bitcast(val_scr[...], I32)
        return jnp.where(bits < 0, bits ^ jnp.int32(0x7FFFFFFF), bits)

    hi_scr[...] = lax.shift_right_arithmetic(keys(), jnp.int32(16)).astype(I16)

    def search(ref, need):
        def step(it, acc):
            cand = acc + lax.shift_left(jnp.int32(1), jnp.int32(15) - it)
            cnt = _count16(ref[...] >= cand.astype(I16))
            return jnp.where(cnt >= need, cand, acc)
        return lax.fori_loop(0, 16, step, jnp.full((1, tq), half_min, I32))

    hi = search(hi_scr, top_k)
    hi16 = hi.astype(I16)
    need = top_k - _count16(hi_scr[...] > hi16)
    low = ((keys() & jnp.int32(0xFFFF)) + half_min).astype(I16)
    lo_scr[...] = jnp.where(hi_scr[...] == hi16, low, half_min.astype(I16))
    lo = search(lo_scr, need)
    return _key_to_float(lax.shift_left(hi, jnp.int32(16)) | ((lo - half_min) & jnp.int32(0xFFFF)))


def _kth_largest_float(val_scr, *, top_k):
    tq = val_scr.shape[1]

    def value_step(it, lo):
        cand = lo ^ lax.shift_left(jnp.int32(1), jnp.int32(31) - it)
        cnt = _key_count(val_scr[...] >= _key_to_float(cand))
        return jnp.where(cnt >= top_k, cand, lo)

    return _key_to_float(lax.fori_loop(0, 32, value_step, jnp.full((1, tq), INT_MIN, I32)))


def _topk_bias(score, causal, val_scr, idx_scr, hi_scr, lo_scr, thr_scr, *, top_k):
    s, tq = score.shape
    key_idx = lax.broadcasted_iota(I32, (s, tq), 0)
    val_scr[...] = jnp.where(causal, score, -jnp.inf)

    thr_scr[...] = jnp.broadcast_to(_kth_largest_fast(val_scr, hi_scr, lo_scr, top_k=top_k), thr_scr.shape)
    thr0 = thr_scr[0:1, :]
    verified = jnp.logical_and(jnp.min(_key_count(val_scr[...] >= thr0)) >= top_k,
                               jnp.max(_key_count(val_scr[...] > thr0)) < top_k)

    @pl.when(jnp.logical_not(verified))
    def _():
        thr_scr[...] = jnp.broadcast_to(_kth_largest_float(val_scr, top_k=top_k), thr_scr.shape)

    thr = thr_scr[0:1, :]
    n_ge = _key_count(val_scr[...] >= thr)
    excess = jnp.max(n_ge) > top_k

    @pl.when(jnp.logical_not(excess))
    def _():
        val_scr[...] = jnp.where(val_scr[...] >= thr, 0.0, NEG_INF)

    @pl.when(excess)
    def _():
        vals = val_scr[...]
        above = vals > thr
        need = top_k - _key_count(above)
        idx_scr[...] = jnp.where(vals == thr, key_idx, jnp.int32(IDX_BIG))
        n_bits = max(1, int(math.ceil(math.log2(s))))

        def index_step(it, p):
            cand = p + lax.shift_left(jnp.int32(1), jnp.int32(n_bits - 1) - it)
            cnt = _key_count(idx_scr[...] < cand)
            return jnp.where(cnt < need, cand, p)

        p = lax.fori_loop(0, n_bits, index_step, jnp.zeros((1, tq), I32))
        keep_tie = idx_scr[...] <= p
        keep = jnp.where(above, 0.0, jnp.where(keep_tie, 0.0, NEG_INF))
        val_scr[...] = jnp.where(causal, keep, NEG_INF)

    return val_scr[...]


def _with_ones_rows(v_t):
    return jnp.concatenate([v_t, jnp.ones((BF16_ROWS, v_t.shape[1]), BF16)], axis=0)


def _dsa_kernel(qa_ref, qi_ref, wit_ref, ka_ref, vat_ref, ki_ref, o_ref, *scratch, s_len, q_start, top_k):
    tq = o_ref.shape[1]
    key_idx = lax.broadcasted_iota(I32, (s_len, tq), 0)
    q_pos = lax.broadcasted_iota(I32, (s_len, tq), 1) + q_start
    causal = key_idx <= q_pos
    if s_len > top_k:
        ki = ki_ref[...]
        w_t = wit_ref[...]
        score = jnp.zeros((s_len, tq), F32)
        g = DSA_HEAD_GROUP
        for h0 in range(0, IDX_HEADS, g):
            dots = lax.dot_general(ki, qi_ref[h0:h0 + g].reshape(g * tq, HEAD_DIM), _CONTRACT_LAST,
                                   preferred_element_type=F32)
            for i in range(g):
                score = score + w_t[h0 + i:h0 + i + 1, :] * jnp.maximum(dots[:, i * tq:(i + 1) * tq], 0.0)
        bias = _topk_bias(score, causal, *scratch, top_k=top_k)
    else:
        bias = jnp.where(causal, 0.0, NEG_INF)
    ka = ka_ref[...]
    v_ext = _with_ones_rows(vat_ref[...])
    g = DSA_HEAD_GROUP
    bias_g = jnp.concatenate([bias] * g, axis=1)
    for h0 in range(0, A_HEADS, g):
        q_g = qa_ref[h0:h0 + g].reshape(g * tq, HEAD_DIM)
        s_t = lax.dot_general(ka, q_g, _CONTRACT_LAST, preferred_element_type=F32) + bias_g
        m = jnp.max(s_t, axis=0, keepdims=True)
        p = jnp.exp2(s_t - m).astype(BF16)
        r = jnp.dot(v_ext, p, preferred_element_type=F32)
        o = (r[0:HEAD_DIM] * (1.0 / r[HEAD_DIM:HEAD_DIM + 1])).astype(BF16)
        for i in range(g):
            o_ref[(h0 + i) * HEAD_DIM:(h0 + i + 1) * HEAD_DIM, :] = o[:, i * tq:(i + 1) * tq]


def _dsa_bucket(qa, qi, wi_t, ka, va_t, ki, *, j, top_k):
    batch, _, seq, _ = qa.shape
    tq = ATTN_TQ
    s_len = (j + 1) * tq
    hm = pl.BlockSpec((None, 8, tq, HEAD_DIM), lambda b: (b, 0, j, 0))
    kv = pl.BlockSpec((None, s_len, HEAD_DIM), lambda b: (b, 0, 0))
    scratch = [pltpu.VMEM((s_len, tq), F32), pltpu.VMEM((s_len, tq), I32), pltpu.VMEM((s_len, tq), I16),
               pltpu.VMEM((s_len, tq), I16), pltpu.VMEM((8, tq), F32)] if s_len > top_k else []
    return pl.pallas_call(
        functools.partial(_dsa_kernel, s_len=s_len, q_start=j * tq, top_k=top_k),
        grid=(batch,),
        in_specs=[hm, hm, pl.BlockSpec((None, IDX_HEADS, tq), lambda b: (b, 0, j)), kv,
                  pl.BlockSpec((None, HEAD_DIM, s_len), lambda b: (b, 0, 0)), kv],
        out_specs=pl.BlockSpec((None, A_HEADS * HEAD_DIM, tq), lambda b: (b, 0, 0)),
        out_shape=jax.ShapeDtypeStruct((batch, A_HEADS * HEAD_DIM, tq), BF16),
        scratch_shapes=scratch,
        compiler_params=_cparams(1),
        name=f"dsa_{j}",
    )(qa, qi, wi_t, ka, va_t, ki)


def _causal_attend(q, k_ref, v_ext, c, j, tri_bias):
    tq = ATTN_TQ
    off = j * tq
    s_d = lax.dot_general(q, k_ref[c, off:off + tq, :], _CONTRACT_LAST, preferred_element_type=F32) + tri_bias
    m = jnp.max(s_d, axis=1, keepdims=True)
    if j > 0:
        s_o = lax.dot_general(q, k_ref[c, 0:off, :], _CONTRACT_LAST, preferred_element_type=F32)
        m = jnp.maximum(m, jnp.max(s_o, axis=1, keepdims=True))
    r = jnp.dot(jnp.exp2(s_d - m).astype(BF16), v_ext[off:off + tq, :], preferred_element_type=F32)
    if j > 0:
        r = r + jnp.dot(jnp.exp2(s_o - m).astype(BF16), v_ext[0:off, :], preferred_element_type=F32)
    return r[:, :B_VDIM] * (1.0 / r[:, B_VDIM:])


def _diff_kernel(lam_ref, subln_ref, q_ref, k_ref, v_ref, o_ref, *, lambda_init):
    seq = o_ref.shape[0]
    tq = ATTN_TQ
    lv = lam_ref[...]
    lam = (jnp.exp(jnp.sum(lv[0:1] * lv[1:2], axis=1, keepdims=True))
           - jnp.exp(jnp.sum(lv[2:3] * lv[3:4], axis=1, keepdims=True)) + lambda_init)
    row = lax.broadcasted_iota(I32, (tq, tq), 0)
    col = lax.broadcasted_iota(I32, (tq, tq), 1)
    tri_bias = jnp.where(col <= row, 0.0, NEG_INF)
    subln = subln_ref[...]
    v = v_ref[...]
    v_ext = jnp.concatenate([v, jnp.ones(v.shape, BF16)], axis=1)
    for j in range(seq // tq):
        rows = slice(j * tq, (j + 1) * tq)
        a = (_causal_attend(q_ref[0, rows, :], k_ref, v_ext, 0, j, tri_bias)
             - lam * _causal_attend(q_ref[1, rows, :], k_ref, v_ext, 1, j, tri_bias))
        y = _rms(a, subln, SUBLN_EPS) * (1.0 - lambda_init)
        o_ref[rows, :] = y.astype(BF16)


def _diff_attention(lam_vecs, subln, qb, kb, vb, *, lambda_init):
    batch, _, seq, _ = qb.shape
    qk = lambda: pl.BlockSpec((None, 2, seq, HEAD_DIM), lambda b, h: (b, h, 0, 0))
    vo = lambda: pl.BlockSpec((None, seq, B_VDIM), lambda b, h: (b, 0, h))
    return pl.pallas_call(
        functools.partial(_diff_kernel, lambda_init=lambda_init),
        grid=(batch, B_HEADS),
        in_specs=[pl.BlockSpec(lam_vecs.shape, lambda b, h: (0, 0)),
                  pl.BlockSpec(subln.shape, lambda b, h: (0, 0)),
                  qk(), qk(), vo()],
        out_specs=vo(),
        out_shape=jax.ShapeDtypeStruct((batch, seq, B_HEADS * B_VDIM), BF16),
        compiler_params=_cparams(2),
        name="diff",
    )(lam_vecs, subln, qb, kb, vb)


def _unroll_for(trips):
    return max(u for u in range(1, DIL_MAX_UNROLL + 1) if trips % u == 0)


def _dil_kernel(q_ref, k_ref, v_ref, o_ref, m_scr, l_scr, acc_scr, bias_scr):
    seq = q_ref.shape[0]
    span = DIL_SPAN
    lane = lax.broadcasted_iota(I32, (span, LANES), 1)
    head0 = lane < HEAD_DIM

    qi = lax.broadcasted_iota(I32, (2 * span, 2 * span), 0) & (span - 1)
    kj = lax.broadcasted_iota(I32, (2 * span, 2 * span), 1)
    dist = kj - qi
    bias_scr[...] = jnp.where(jnp.where(kj < span, dist, span - dist) >= 0, 0.0, NEG_INF)

    def rows_at(start, d):
        return pl.ds(start, span) if d == 1 else pl.ds(start, span, stride=d)

    def tile(t, *, d, blk0, first):
        r = t % d
        blk = blk0 + t // d
        start = r + blk * (span * d)
        if d == 1:
            start = pl.multiple_of(start, span)
        rows = rows_at(start, d)
        qb = q_ref[rows, :]
        qs = jnp.concatenate([jnp.where(head0, qb, 0.0), jnp.where(head0, 0.0, qb)], axis=0).astype(BF16)
        kk = k_ref[rows, :]
        vv = v_ref[rows, :]
        has_prev = blk0 > 0
        if has_prev:
            prev = rows_at(start - span * d, d)
            kk = jnp.concatenate([k_ref[prev, :], kk], axis=0)
            vv = jnp.concatenate([v_ref[prev, :], vv], axis=0)
            bias = bias_scr[...]
        else:
            bias = bias_scr[:, span:]
        s = lax.dot_general(qs, kk.astype(BF16), _CONTRACT_LAST, preferred_element_type=F32) + bias
        m = jnp.max(s, axis=1, keepdims=True)
        p = jnp.exp2(s - m).astype(BF16)
        v_ext = jnp.concatenate([vv.astype(BF16), jnp.ones(vv.shape, BF16)], axis=1)
        pv = jnp.dot(p, v_ext, preferred_element_type=F32)
        acc_t = jnp.where(head0, pv[:span, :LANES], pv[span:, :LANES])
        l_t = jnp.where(head0, pv[:span, LANES:], pv[span:, LANES:])
        m_t = jnp.where(head0, m[:span], m[span:])
        if first:
            m_scr[rows, :] = m_t
            l_scr[rows, :] = l_t
            acc_scr[rows, :] = acc_t
        else:
            m_o = m_scr[rows, :]
            m_n = jnp.maximum(m_o, m_t)
            ea = jnp.exp2(m_o - m_n)
            eb = jnp.exp2(m_t - m_n)
            l_scr[rows, :] = ea * l_scr[rows, :] + eb * l_t
            acc_scr[rows, :] = ea * acc_scr[rows, :] + eb * acc_t
            m_scr[rows, :] = m_n

    for pi, (window, d) in enumerate(reversed(C_PATTERNS)):
        n_blocks = (seq // d) // span
        first = pi == 0

        def no_prev(t, carry, d=d, first=first):
            tile(t, d=d, blk0=0, first=first)
            return carry

        def with_prev(t, carry, d=d, first=first):
            tile(t, d=d, blk0=1, first=first)
            return carry

        lax.fori_loop(0, d, no_prev, 0, unroll=_unroll_for(d))
        if n_blocks > 1:
            trips = d * (n_blocks - 1)
            lax.fori_loop(0, trips, with_prev, 0, unroll=_unroll_for(trips))

    o_ref[...] = (acc_scr[...] * (1.0 / l_scr[...])).astype(BF16)


def _dilated(q, k, v):
    batch, seq, d = q.shape
    n_pairs = d // LANES
    blk = lambda: pl.BlockSpec((None, seq, LANES), lambda b, hp: (b, 0, hp))
    return pl.pallas_call(
        _dil_kernel,
        grid=(batch, n_pairs),
        in_specs=[blk(), blk(), blk()],
        out_specs=blk(),
        out_shape=jax.ShapeDtypeStruct((batch, seq, d), BF16),
        scratch_shapes=[pltpu.VMEM((seq, LANES), F32)] * 3 + [pltpu.VMEM((2 * DIL_SPAN, 2 * DIL_SPAN), F32)],
        compiler_params=_cparams(2),
        name="dilated",
    )(q, k, v)


def _rope_tables(seq):
    inv = 1.0 / (ROPE_THETA ** (jnp.arange(0, HEAD_DIM, 2, dtype=F32) / HEAD_DIM))
    ang = jnp.arange(seq, dtype=F32)[:, None] * inv[None, :]
    cos, sin = jnp.cos(ang), jnp.sin(ang)
    cos_t = jnp.concatenate([cos, cos, cos, cos], axis=1)
    sin_t = jnp.concatenate([-sin, sin, -sin, sin], axis=1)
    return cos_t, sin_t


def _pack_even_w_in(w):
    cuts = np.cumsum([0, 512, 64, 64, 512, 64, 8, 512, 512, 512])
    qa, ka, va, qi, ki, wi, qb, kb, vb = [w[:, cuts[i]:cuts[i + 1]] for i in range(9)]
    w_cols = jnp.concatenate([qa, qi, qb, kb, ka, ki, vb], axis=1).astype(BF16)
    w_t = jnp.concatenate([va, wi], axis=1).T
    pad = (-w_t.shape[0]) % BF16_ROWS
    w_t = jnp.pad(w_t, ((0, pad), (0, 0))).astype(BF16)
    assert w_cols.shape[1] == EVEN_ROPE_COLS + B_HEADS * B_VDIM and w_t.shape[0] >= EVEN_T_ROWS
    return w_cols, w_t


def kernel(x, p, norm_ffn_a, ffn_a_wg, ffn_a_wu, ffn_a_wd, norm_mix, norm_ffn_b, ffn_b_wg, ffn_b_wu,
           ffn_b_wd, norm_ple, ple_gate, ple_proj, even_w_in, even_w_out, diff_lambda_q1,
           diff_lambda_k1, diff_lambda_q2, diff_lambda_k2, diff_subln, odd_w_in, odd_w_out, final_norm):
    batch, seq, d_model = x.shape
    depth = p.shape[0]
    assert seq % ATTN_TQ == 0 and (batch * seq) % TOKEN_TILE == 0 and seq % TOKEN_TILE == 0
    assert seq == C_PATTERNS[-1][0], "dilated kernel assumes the widest window spans the sequence"
    top_k = min(A_TOPK_MAX, seq // 4)
    cos_t, sin_t = _rope_tables(seq)
    gain = lambda g: g.reshape(1, -1).astype(F32)
    n_buckets = seq // ATTN_TQ
    r3 = lambda a: a.reshape(batch, seq, a.shape[-1])
    p_all = p.reshape(depth, batch * seq, p.shape[-1])
    f32 = lambda w: w.astype(F32)
    ffn_a = (f32(ffn_a_wg), f32(ffn_a_wu), f32(ffn_a_wd))
    ffn_b = (f32(ffn_b_wg), f32(ffn_b_wu), f32(ffn_b_wd))

    h = x.reshape(batch * seq, d_model)
    for i in range(depth):
        ga, gm = gain(norm_ffn_a[i]), gain(norm_mix[i])
        if i % 2 == 0:
            e = i // 2
            lambda_init = 0.8 - 0.6 * math.exp(-0.3 * i)
            (ha, qa, qi, qb, kb, ka, ki, vb, va_t, wi_t) = _pre_even(
                h, ga, gm, *_pack_even_w_in(even_w_in[e]), cos_t, sin_t, *ffn_a, layer=i, batch=batch, seq=seq)
            ka, ki, vb = r3(ka), r3(ki), r3(vb)
            lam_vecs = jnp.stack([diff_lambda_q1[e], diff_lambda_k1[e],
                                  diff_lambda_q2[e], diff_lambda_k2[e]]).astype(F32)
            out_a = [_dsa_bucket(qa, qi, wi_t, ka, va_t, ki, j=j, top_k=top_k) for j in range(n_buckets)]
            out_b = _diff_attention(lam_vecs, gain(diff_subln[e]), qb, kb, vb, lambda_init=lambda_init)
            mixes = [jnp.concatenate(out_a, axis=2), out_b.reshape(batch * seq, -1)]
            mix_transposed = (True, False)
            w_out_all, mixer_layer = f32(even_w_out), e
        else:
            o = i // 2
            ha, q, k, v = _pre_odd(h, ga, gm, cos_t, sin_t, *ffn_a, f32(odd_w_in),
                                   layer=i, mixer_layer=o, seq=seq)
            mixes = [_dilated(r3(q), r3(k), r3(v)).reshape(batch * seq, -1)]
            mix_transposed = (False,)
            w_out_all, mixer_layer = f32(odd_w_out), o
        gf = gain(final_norm) if i == depth - 1 else None
        h = _post(ha, mixes, gain(norm_ffn_b[i]), gain(norm_ple[i]), p_all, gf,
                  w_out_all, *ffn_b, f32(ple_gate), f32(ple_proj),
                  mix_transposed=mix_transposed, layer=i, mixer_layer=mixer_layer, seq=seq)
    return h.reshape(batch, seq, d_model)
```

```python
import functools
import math

import jax
import jax.numpy as jnp
import numpy as np
from jax import lax
from jax.experimental import pallas as pl
from jax.experimental.pallas import tpu as pltpu

F32 = jnp.float32
BF16 = jnp.bfloat16
I32 = jnp.int32
I16 = jnp.int16

HEAD_DIM = 64
ROPE_THETA = 10000.0
NORM_EPS = 1e-6
SUBLN_EPS = 1e-5
NEG_INF = -1e30
A_HEADS = 8
A_TOPK_MAX = 256
IDX_HEADS = 8
B_HEADS = 4
B_VDIM = 2 * HEAD_DIM
C_PATTERNS = ((128, 1), (512, 4), (2048, 16))
DIL_SPAN = 128

LANES = 128
BF16_ROWS = 16
VMEM_LIMIT_BYTES = 56 * 1024 * 1024
TOKEN_TILE = 256
ATTN_TQ = 256
DIL_MAX_UNROLL = 16
WEIGHT_STAGE_ELEMS = 192 * 1024
WEIGHT_STAGE_SLOTS = 6
DSA_HEAD_GROUP = 4

ATTN_Q_SCALE = (HEAD_DIM ** -0.5) * math.log2(math.e)

INT_MIN = -(2 ** 31)
KEY_NEG_INF = INT_MIN + 0x7FFFFF
KEY_POS_INF = 0x7F800000
IDX_BIG = 2 ** 30

EVEN_ROPE_COLS = 4 * 512 + 2 * HEAD_DIM
EVEN_T_ROWS = HEAD_DIM + IDX_HEADS

_CONTRACT_LAST = (((1,), (1,)), ((), ()))
_CONTRACT_FIRST = (((0,), (0,)), ((), ()))


def _cparams(n_grid):
    return pltpu.CompilerParams(dimension_semantics=("arbitrary",) * n_grid,
                                vmem_limit_bytes=VMEM_LIMIT_BYTES)


def _resident(shape):
    nd = len(shape)
    return pl.BlockSpec(shape, lambda *_: (0,) * nd, pipeline_mode=pl.Buffered(1))


_HBM = pl.BlockSpec(memory_space=pl.ANY)


def _stage_rows(rows, cols):
    return max(r for r in range(BF16_ROWS, rows + 1, BF16_ROWS)
               if rows % r == 0 and r * cols <= WEIGHT_STAGE_ELEMS)


def _weight_scratch(shapes):
    ns = WEIGHT_STAGE_SLOTS
    stage = (ns, max(_stage_rows(r, c) for r, c in shapes), max(c for _, c in shapes))
    return [pltpu.VMEM(s, BF16) for s in shapes] + [pltpu.VMEM(stage, F32), pltpu.SemaphoreType.DMA((ns,))]


def _fetch_weights(layer, hbm_refs, vmem_refs, stage, sem):
    @pl.when(pl.program_id(0) == 0)
    def _():
        for w_hbm, w_vmem in zip(hbm_refs, vmem_refs):
            rows, cols = w_vmem.shape
            rc = _stage_rows(rows, cols)
            n = rows // rc
            ns = stage.shape[0]
            ahead = min(ns - 1, n)

            def copy(c, w_hbm=w_hbm, rc=rc, cols=cols):
                slot = c % ns
                return pltpu.make_async_copy(w_hbm.at[layer, pl.ds(c * rc, rc), :],
                                             stage.at[slot, pl.ds(0, rc), pl.ds(0, cols)], sem.at[slot])

            def body(c, carry, copy=copy, w_vmem=w_vmem, rc=rc, cols=cols, n=n, ahead=ahead):
                @pl.when(c + ahead < n)
                def _():
                    copy(c + ahead).start()
                copy(c).wait()
                w_vmem[pl.ds(pl.multiple_of(c * rc, rc), rc), :] = stage[c % ns, 0:rc, 0:cols].astype(BF16)
                return carry

            for c in range(ahead):
                copy(c).start()
            lax.fori_loop(0, n, body, 0)


def _rms(x, g, eps):
    ms = jnp.mean(x * x, axis=-1, keepdims=True)
    return x * lax.rsqrt(ms + eps) * g


def _ffn_half(h, g, wg_ref, wu_ref, wd_ref):
    xn = _rms(h, g, NORM_EPS).astype(BF16)
    a = jnp.dot(xn, wg_ref[...], preferred_element_type=F32)
    b = jnp.dot(xn, wu_ref[...], preferred_element_type=F32)
    mid = (a / (1.0 + jnp.exp(-a)) * b).astype(BF16)
    return h + 0.5 * jnp.dot(mid, wd_ref[...], preferred_element_type=F32)


def _rope(y, cos_t, sin_t):
    w = y.shape[1]
    n = w // LANES
    c = jnp.concatenate([cos_t] * n, axis=1) if n > 1 else cos_t
    s = jnp.concatenate([sin_t] * n, axis=1) if n > 1 else sin_t
    lane = lax.broadcasted_iota(I32, y.shape, 1)
    first_half = (lane & (HEAD_DIM // 2)) == 0
    partner = jnp.where(first_half,
                        pltpu.roll(y, w - HEAD_DIM // 2, 1),
                        pltpu.roll(y, HEAD_DIM // 2, 1))
    return y * c + partner * s


def _pre_even_kernel(h_ref, ga_ref, gm_ref, wr_ref, wt_ref, cos_ref, sin_ref, wg_hbm, wu_hbm, wd_hbm,
                     ha_ref, qa_ref, qi_ref, qb_ref, kb_ref, ka_ref, ki_ref, vb_ref, vat_ref, wit_ref,
                     wg_ref, wu_ref, wd_ref, stage, sem, *, layer, idx_scale):
    _fetch_weights(layer, (wg_hbm, wu_hbm, wd_hbm), (wg_ref, wu_ref, wd_ref), stage, sem)
    ha = _ffn_half(h_ref[...], ga_ref[...], wg_ref, wu_ref, wd_ref)
    ha_ref[...] = ha
    xn = _rms(ha, gm_ref[...], NORM_EPS).astype(BF16)
    y = jnp.dot(xn, wr_ref[...], preferred_element_type=F32)
    yr = _rope(y[:, :EVEN_ROPE_COLS], cos_ref[...], sin_ref[...])
    vb_ref[...] = y[:, EVEN_ROPE_COLS:EVEN_ROPE_COLS + B_HEADS * B_VDIM].astype(BF16)
    for hd in range(8):
        lo = hd * HEAD_DIM
        qa_ref[hd] = (yr[:, lo:lo + HEAD_DIM] * ATTN_Q_SCALE).astype(BF16)
        qi_ref[hd] = yr[:, 512 + lo:512 + lo + HEAD_DIM].astype(BF16)
        qb_ref[hd] = (yr[:, 1024 + lo:1024 + lo + HEAD_DIM] * ATTN_Q_SCALE).astype(BF16)
        kb_ref[hd] = yr[:, 1536 + lo:1536 + lo + HEAD_DIM].astype(BF16)
    ka_ref[...] = yr[:, 2048:2112].astype(BF16)
    ki_ref[...] = yr[:, 2112:2176].astype(BF16)
    yt = lax.dot_general(wt_ref[...], xn, _CONTRACT_LAST, preferred_element_type=F32)
    vat_ref[...] = yt[0:HEAD_DIM].astype(BF16)
    wit_ref[...] = yt[HEAD_DIM:HEAD_DIM + IDX_HEADS] * idx_scale


def _pre_odd_kernel(h_ref, ga_ref, gm_ref, cos_ref, sin_ref, wg_hbm, wu_hbm, wd_hbm, win_hbm,
                    ha_ref, q_ref, k_ref, v_ref,
                    wg_ref, wu_ref, wd_ref, win_ref, stage, sem, *, layer, mixer_layer):
    _fetch_weights(layer, (wg_hbm, wu_hbm, wd_hbm), (wg_ref, wu_ref, wd_ref), stage, sem)
    _fetch_weights(mixer_layer, (win_hbm,), (win_ref,), stage, sem)
    ha = _ffn_half(h_ref[...], ga_ref[...], wg_ref, wu_ref, wd_ref)
    ha_ref[...] = ha
    xn = _rms(ha, gm_ref[...], NORM_EPS).astype(BF16)
    y = jnp.dot(xn, win_ref[...], preferred_element_type=F32)
    d = q_ref.shape[1]
    yr = _rope(y[:, :2 * d], cos_ref[...], sin_ref[...])
    q_ref[...] = yr[:, :d] * ATTN_Q_SCALE
    k_ref[...] = yr[:, d:2 * d]
    v_ref[...] = y[:, 2 * d:3 * d]


def _token_specs(tm, seq_tiles):
    row = lambda w: pl.BlockSpec((tm, w), lambda i: (i, 0))
    rope = pl.BlockSpec((tm, LANES), lambda i: (i % seq_tiles, 0))
    return row, rope


def _pre_even(h, ga, gm, w_cols, w_t, cos_t, sin_t, wg_all, wu_all, wd_all, *, layer, batch, seq):
    m, d = h.shape
    tm = TOKEN_TILE
    nt = seq // tm
    row, rope = _token_specs(tm, nt)
    hm = lambda: pl.BlockSpec((None, 8, tm, HEAD_DIM), lambda i: (i // nt, 0, i % nt, 0))
    hm_shape = jax.ShapeDtypeStruct((batch, 8, seq, HEAD_DIM), BF16)
    tr = lambda rows: pl.BlockSpec((None, rows, tm), lambda i: (i // nt, 0, i % nt))
    idx_scale = (HEAD_DIM ** -0.5) * (IDX_HEADS ** -0.5)
    return pl.pallas_call(
        functools.partial(_pre_even_kernel, layer=layer, idx_scale=idx_scale),
        grid=(m // tm,),
        in_specs=[row(d), _resident(ga.shape), _resident(gm.shape), _resident(w_cols.shape),
                  _resident(w_t.shape), rope, rope, _HBM, _HBM, _HBM],
        out_specs=[row(d), hm(), hm(), hm(), hm(), row(HEAD_DIM), row(HEAD_DIM),
                   row(B_HEADS * B_VDIM), tr(HEAD_DIM), tr(IDX_HEADS)],
        out_shape=[jax.ShapeDtypeStruct((m, d), F32), hm_shape, hm_shape, hm_shape, hm_shape,
                   jax.ShapeDtypeStruct((m, HEAD_DIM), BF16), jax.ShapeDtypeStruct((m, HEAD_DIM), BF16),
                   jax.ShapeDtypeStruct((m, B_HEADS * B_VDIM), BF16),
                   jax.ShapeDtypeStruct((batch, HEAD_DIM, seq), BF16),
                   jax.ShapeDtypeStruct((batch, IDX_HEADS, seq), F32)],
        scratch_shapes=_weight_scratch([wg_all.shape[1:], wu_all.shape[1:], wd_all.shape[1:]]),
        compiler_params=_cparams(1),
        name="pre_even",
    )(h, ga, gm, w_cols, w_t, cos_t, sin_t, wg_all, wu_all, wd_all)


def _pre_odd(h, ga, gm, cos_t, sin_t, wg_all, wu_all, wd_all, win_all, *, layer, mixer_layer, seq):
    m, d = h.shape
    tm = TOKEN_TILE
    nt = seq // tm
    row, rope = _token_specs(tm, nt)
    out = jax.ShapeDtypeStruct((m, d), F32)
    return pl.pallas_call(
        functools.partial(_pre_odd_kernel, layer=layer, mixer_layer=mixer_layer),
        grid=(m // tm,),
        in_specs=[row(d), _resident(ga.shape), _resident(gm.shape), rope, rope, _HBM, _HBM, _HBM, _HBM],
        out_specs=[row(d), row(d), row(d), row(d)],
        out_shape=[out, out, out, out],
        scratch_shapes=_weight_scratch([wg_all.shape[1:], wu_all.shape[1:], wd_all.shape[1:], win_all.shape[1:]]),
        compiler_params=_cparams(1),
        name="pre_odd",
    )(h, ga, gm, cos_t, sin_t, wg_all, wu_all, wd_all, win_all)


def _post_kernel(*refs, n_mix, mix_transposed, final, layer, mixer_layer):
    refs = list(refs)
    h_ref = refs.pop(0)
    mix_refs = [refs.pop(0) for _ in range(n_mix)]
    gb_ref, gp_ref, p_ref = (refs.pop(0) for _ in range(3))
    gf_ref = refs.pop(0) if final else None
    wout_hbm, wg_hbm, wu_hbm, wd_hbm, pg_hbm, pp_hbm = (refs.pop(0) for _ in range(6))
    o_ref = refs.pop(0)
    wout_ref, wg_ref, wu_ref, wd_ref, pg_ref, pp_ref, stage, sem = refs
    _fetch_weights(mixer_layer, (wout_hbm,), (wout_ref,), stage, sem)
    _fetch_weights(layer, (wg_hbm, wu_hbm, wd_hbm, pg_hbm, pp_hbm), (wg_ref, wu_ref, wd_ref, pg_ref, pp_ref),
                   stage, sem)
    h1 = h_ref[...]
    row0 = 0
    for r, transposed in zip(mix_refs, mix_transposed):
        if transposed:
            w = wout_ref[row0:row0 + r.shape[0], :]
            h1 = h1 + lax.dot_general(r[...], w, _CONTRACT_FIRST, preferred_element_type=F32)
            row0 += r.shape[0]
        else:
            w = wout_ref[row0:row0 + r.shape[1], :]
            h1 = h1 + jnp.dot(r[...], w, preferred_element_type=F32)
            row0 += r.shape[1]
    h2 = _ffn_half(h1, gb_ref[...], wg_ref, wu_ref, wd_ref)
    z = jnp.dot(_rms(h2, gp_ref[...], NORM_EPS).astype(BF16), pg_ref[...], preferred_element_type=F32)
    gate = 1.0 / (1.0 + jnp.exp(-z))
    emb = jnp.dot(p_ref[...].astype(BF16), pp_ref[...], preferred_element_type=F32)
    h3 = h2 + gate * emb
    if final:
        h3 = _rms(h3, gf_ref[...], NORM_EPS)
    o_ref[...] = h3


def _post(h, mixes, gb, gp, p_all, gf, wout_all, wg_all, wu_all, wd_all, pg_all, pp_all,
          *, mix_transposed, layer, mixer_layer, seq):
    m, d = h.shape
    tm = TOKEN_TILE
    nt = seq // tm
    row = lambda w: pl.BlockSpec((tm, w), lambda i: (i, 0))
    mix_specs = [pl.BlockSpec((None, x.shape[1], tm), lambda i: (i // nt, 0, i % nt)) if t else row(x.shape[1])
                 for x, t in zip(mixes, mix_transposed)]
    final = gf is not None
    weights = [wout_all, wg_all, wu_all, wd_all, pg_all, pp_all]
    args = [h, *mixes, gb, gp, p_all] + ([gf] if final else []) + weights
    specs = ([row(d), *mix_specs, _resident(gb.shape), _resident(gp.shape),
              pl.BlockSpec((None, tm, p_all.shape[2]), lambda i: (layer, i, 0))]
             + ([_resident(gf.shape)] if final else []) + [_HBM] * len(weights))
    return pl.pallas_call(
        functools.partial(_post_kernel, n_mix=len(mixes), mix_transposed=mix_transposed, final=final,
                          layer=layer, mixer_layer=mixer_layer),
        grid=(m // tm,),
        in_specs=specs,
        out_specs=row(d),
        out_shape=jax.ShapeDtypeStruct((m, d), F32),
        scratch_shapes=_weight_scratch([w.shape[1:] for w in weights]),
        compiler_params=_cparams(1),
        name="post",
    )(*args)


def _key_count(mask):
    return jnp.sum(jnp.where(mask, 1.0, 0.0), axis=0, keepdims=True)


def _key_to_float(key):
    key = jnp.clip(key, jnp.int32(KEY_NEG_INF), jnp.int32(KEY_POS_INF))
    return pltpu.bitcast(jnp.where(key < 0, key ^ jnp.int32(0x7FFFFFFF), key), F32)


def _count16(mask):
    s = mask.shape[0]
    ones = jnp.where(mask, jnp.ones((), I16), jnp.zeros((), I16))
    parts = [ones[i:i + BF16_ROWS] for i in range(0, s, BF16_ROWS)]
    while len(parts) > 1:
        nxt = [parts[i] + parts[i + 1] for i in range(0, len(parts) - 1, 2)]
        parts = nxt + ([parts[-1]] if len(parts) % 2 else [])
    return jnp.sum(parts[0].astype(F32), axis=0, keepdims=True)


def _kth_largest_fast(val_scr, hi_scr, lo_scr, *, top_k):
    tq = val_scr.shape[1]
    half_min = jnp.int32(-(2 ** 15))

    def keys():
        bits = pltpu.bitcast(val_scr[...], I32)
        return jnp.where(bits < 0, bits ^ jnp.int32(0x7FFFFFFF), bits)

    hi_scr[...] = lax.shift_right_arithmetic(keys(), jnp.int32(16)).astype(I16)

    def search(ref, need):
        def step(it, acc):
            cand = acc + lax.shift_left(jnp.int32(1), jnp.int32(15) - it)
            cnt = _count16(ref[...] >= cand.astype(I16))
            return jnp.where(cnt >= need, cand, acc)
        return lax.fori_loop(0, 16, step, jnp.full((1, tq), half_min, I32))

    hi = search(hi_scr, top_k)
    hi16 = hi.astype(I16)
    need = top_k - _count16(hi_scr[...] > hi16)
    low = ((keys() & jnp.int32(0xFFFF)) + half_min).astype(I16)
    lo_scr[...] = jnp.where(hi_scr[...] == hi16, low, half_min.astype(I16))
    lo = search(lo_scr, need)
    return _key_to_float(lax.shift_left(hi, jnp.int32(16)) | ((lo - half_min) & jnp.int32(0xFFFF)))


def _kth_largest_float(val_scr, *, top_k):
    tq = val_scr.shape[1]

    def value_step(it, lo):
        cand = lo ^ lax.shift_left(jnp.int32(1), jnp.int32(31) - it)
        cnt = _key_count(val_scr[...] >= _key_to_float(cand))
        return jnp.where(cnt >= top_k, cand, lo)

    return _key_to_float(lax.fori_loop(0, 32, value_step, jnp.full((1, tq), INT_MIN, I32)))


def _topk_bias(score, causal, val_scr, idx_scr, hi_scr, lo_scr, thr_scr, *, top_k):
    s, tq = score.shape
    key_idx = lax.broadcasted_iota(I32, (s, tq), 0)
    val_scr[...] = jnp.where(causal, score, -jnp.inf)

    thr_scr[...] = jnp.broadcast_to(_kth_largest_fast(val_scr, hi_scr, lo_scr, top_k=top_k), thr_scr.shape)
    thr0 = thr_scr[0:1, :]
    verified = jnp.logical_and(jnp.min(_key_count(val_scr[...] >= thr0)) >= top_k,
                               jnp.max(_key_count(val_scr[...] > thr0)) < top_k)

    @pl.when(jnp.logical_not(verified))
    def _():
        thr_scr[...] = jnp.broadcast_to(_kth_largest_float(val_scr, top_k=top_k), thr_scr.shape)

    thr = thr_scr[0:1, :]
    n_ge = _key_count(val_scr[...] >= thr)
    excess = jnp.max(n_ge) > top_k

    @pl.when(jnp.logical_not(excess))
    def _():
        val_scr[...] = jnp.where(val_scr[...] >= thr, 0.0, NEG_INF)

    @pl.when(excess)
    def _():
        vals = val_scr[...]
        above = vals > thr
        need = top_k - _key_count(above)
        idx_scr[...] = jnp.where(vals == thr, key_idx, jnp.int32(IDX_BIG))
        n_bits = max(1, int(math.ceil(math.log2(s))))

        def index_step(it, p):
            cand = p + lax.shift_left(jnp.int32(1), jnp.int32(n_bits - 1) - it)
            cnt = _key_count(idx_scr[...] < cand)
            return jnp.where(cnt < need, cand, p)

        p = lax.fori_loop(0, n_bits, index_step, jnp.zeros((1, tq), I32))
        keep_tie = idx_scr[...] <= p
        keep = jnp.where(above, 0.0, jnp.where(keep_tie, 0.0, NEG_INF))
        val_scr[...] = jnp.where(causal, keep, NEG_INF)

    return val_scr[...]


def _with_ones_rows(v_t):
    return jnp.concatenate([v_t, jnp.ones((BF16_ROWS, v_t.shape[1]), BF16)], axis=0)


def _dsa_kernel(qa_ref, qi_ref, wit_ref, ka_ref, vat_ref, ki_ref, o_ref, *scratch, s_len, q_start, top_k):
    tq = o_ref.shape[1]
    key_idx = lax.broadcasted_iota(I32, (s_len, tq), 0)
    q_pos = lax.broadcasted_iota(I32, (s_len, tq), 1) + q_start
    causal = key_idx <= q_pos
    if s_len > top_k:
        ki = ki_ref[...]
        w_t = wit_ref[...]
        score = jnp.zeros((s_len, tq), F32)
        g = DSA_HEAD_GROUP
        for h0 in range(0, IDX_HEADS, g):
            dots = lax.dot_general(ki, qi_ref[h0:h0 + g].reshape(g * tq, HEAD_DIM), _CONTRACT_LAST,
                                   preferred_element_type=F32)
            for i in range(g):
                score = score + w_t[h0 + i:h0 + i + 1, :] * jnp.maximum(dots[:, i * tq:(i + 1) * tq], 0.0)
        bias = _topk_bias(score, causal, *scratch, top_k=top_k)
    else:
        bias = jnp.where(causal, 0.0, NEG_INF)
    ka = ka_ref[...]
    v_ext = _with_ones_rows(vat_ref[...])
    g = DSA_HEAD_GROUP
    bias_g = jnp.concatenate([bias] * g, axis=1)
    for h0 in range(0, A_HEADS, g):
        q_g = qa_ref[h0:h0 + g].reshape(g * tq, HEAD_DIM)
        s_t = lax.dot_general(ka, q_g, _CONTRACT_LAST, preferred_element_type=F32) + bias_g
        m = jnp.max(s_t, axis=0, keepdims=True)
        p = jnp.exp2(s_t - m).astype(BF16)
        r = jnp.dot(v_ext, p, preferred_element_type=F32)
        o = (r[0:HEAD_DIM] * (1.0 / r[HEAD_DIM:HEAD_DIM + 1])).astype(BF16)
        for i in range(g):
            o_ref[(h0 + i) * HEAD_DIM:(h0 + i + 1) * HEAD_DIM, :] = o[:, i * tq:(i + 1) * tq]


def _dsa_bucket(qa, qi, wi_t, ka, va_t, ki, *, j, top_k):
    batch, _, seq, _ = qa.shape
    tq = ATTN_TQ
    s_len = (j + 1) * tq
    hm = pl.BlockSpec((None, 8, tq, HEAD_DIM), lambda b: (b, 0, j, 0))
    kv = pl.BlockSpec((None, s_len, HEAD_DIM), lambda b: (b, 0, 0))
    scratch = [pltpu.VMEM((s_len, tq), F32), pltpu.VMEM((s_len, tq), I32), pltpu.VMEM((s_len, tq), I16),
               pltpu.VMEM((s_len, tq), I16), pltpu.VMEM((8, tq), F32)] if s_len > top_k else []
    return pl.pallas_call(
        functools.partial(_dsa_kernel, s_len=s_len, q_start=j * tq, top_k=top_k),
        grid=(batch,),
        in_specs=[hm, hm, pl.BlockSpec((None, IDX_HEADS, tq), lambda b: (b, 0, j)), kv,
                  pl.BlockSpec((None, HEAD_DIM, s_len), lambda b: (b, 0, 0)), kv],
        out_specs=pl.BlockSpec((None, A_HEADS * HEAD_DIM, tq), lambda b: (b, 0, 0)),
        out_shape=jax.ShapeDtypeStruct((batch, A_HEADS * HEAD_DIM, tq), BF16),
        scratch_shapes=scratch,
        compiler_params=_cparams(1),
        name=f"dsa_{j}",
    )(qa, qi, wi_t, ka, va_t, ki)


def _causal_attend(q, k_ref, v_ext, c, j, tri_bias):
    tq = ATTN_TQ
    off = j * tq
    s_d = lax.dot_general(q, k_ref[c, off:off + tq, :], _CONTRACT_LAST, preferred_element_type=F32) + tri_bias
    m = jnp.max(s_d, axis=1, keepdims=True)
    if j > 0:
        s_o = lax.dot_general(q, k_ref[c, 0:off, :], _CONTRACT_LAST, preferred_element_type=F32)
        m = jnp.maximum(m, jnp.max(s_o, axis=1, keepdims=True))
    r = jnp.dot(jnp.exp2(s_d - m).astype(BF16), v_ext[off:off + tq, :], preferred_element_type=F32)
    if j > 0:
        r = r + jnp.dot(jnp.exp2(s_o - m).astype(BF16), v_ext[0:off, :], preferred_element_type=F32)
    return r[:, :B_VDIM] * (1.0 / r[:, B_VDIM:])


def _diff_kernel(lam_ref, subln_ref, q_ref, k_ref, v_ref, o_ref, *, lambda_init):
    seq = o_ref.shape[0]
    tq = ATTN_TQ
    lv = lam_ref[...]
    lam = (jnp.exp(jnp.sum(lv[0:1] * lv[1:2], axis=1, keepdims=True))
           - jnp.exp(jnp.sum(lv[2:3] * lv[3:4], axis=1, keepdims=True)) + lambda_init)
    row = lax.broadcasted_iota(I32, (tq, tq), 0)
    col = lax.broadcasted_iota(I32, (tq, tq), 1)
    tri_bias = jnp.where(col <= row, 0.0, NEG_INF)
    subln = subln_ref[...]
    v = v_ref[...]
    v_ext = jnp.concatenate([v, jnp.ones(v.shape, BF16)], axis=1)
    for j in range(seq // tq):
        rows = slice(j * tq, (j + 1) * tq)
        a = (_causal_attend(q_ref[0, rows, :], k_ref, v_ext, 0, j, tri_bias)
             - lam * _causal_attend(q_ref[1, rows, :], k_ref, v_ext, 1, j, tri_bias))
        y = _rms(a, subln, SUBLN_EPS) * (1.0 - lambda_init)
        o_ref[rows, :] = y.astype(BF16)


def _diff_attention(lam_vecs, subln, qb, kb, vb, *, lambda_init):
    batch, _, seq, _ = qb.shape
    qk = lambda: pl.BlockSpec((None, 2, seq, HEAD_DIM), lambda b, h: (b, h, 0, 0))
    vo = lambda: pl.BlockSpec((None, seq, B_VDIM), lambda b, h: (b, 0, h))
    return pl.pallas_call(
        functools.partial(_diff_kernel, lambda_init=lambda_init),
        grid=(batch, B_HEADS),
        in_specs=[pl.BlockSpec(lam_vecs.shape, lambda b, h: (0, 0)),
                  pl.BlockSpec(subln.shape, lambda b, h: (0, 0)),
                  qk(), qk(), vo()],
        out_specs=vo(),
        out_shape=jax.ShapeDtypeStruct((batch, seq, B_HEADS * B_VDIM), BF16),
        compiler_params=_cparams(2),
        name="diff",
    )(lam_vecs, subln, qb, kb, vb)


def _unroll_for(trips):
    return max(u for u in range(1, DIL_MAX_UNROLL + 1) if trips % u == 0)


def _dil_kernel(q_ref, k_ref, v_ref, o_ref, m_scr, l_scr, acc_scr, bias_scr):
    seq = q_ref.shape[0]
    span = DIL_SPAN
    lane = lax.broadcasted_iota(I32, (span, LANES), 1)
    head0 = lane < HEAD_DIM

    qi = lax.broadcasted_iota(I32, (2 * span, 2 * span), 0) & (span - 1)
    kj = lax.broadcasted_iota(I32, (2 * span, 2 * span), 1)
    dist = kj - qi
    bias_scr[...] = jnp.where(jnp.where(kj < span, dist, span - dist) >= 0, 0.0, NEG_INF)

    def rows_at(start, d):
        return pl.ds(start, span) if d == 1 else pl.ds(start, span, stride=d)

    def tile(t, *, d, blk0, first):
        r = t % d
        blk = blk0 + t // d
        start = r + blk * (span * d)
        if d == 1:
            start = pl.multiple_of(start, span)
        rows = rows_at(start, d)
        qb = q_ref[rows, :]
        qs = jnp.concatenate([jnp.where(head0, qb, 0.0), jnp.where(head0, 0.0, qb)], axis=0).astype(BF16)
        kk = k_ref[rows, :]
        vv = v_ref[rows, :]
        has_prev = blk0 > 0
        if has_prev:
            prev = rows_at(start - span * d, d)
            kk = jnp.concatenate([k_ref[prev, :], kk], axis=0)
            vv = jnp.concatenate([v_ref[prev, :], vv], axis=0)
            bias = bias_scr[...]
        else:
            bias = bias_scr[:, span:]
        s = lax.dot_general(qs, kk.astype(BF16), _CONTRACT_LAST, preferred_element_type=F32) + bias
        m = jnp.max(s, axis=1, keepdims=True)
        p = jnp.exp2(s - m).astype(BF16)
        v_ext = jnp.concatenate([vv.astype(BF16), jnp.ones(vv.shape, BF16)], axis=1)
        pv = jnp.dot(p, v_ext, preferred_element_type=F32)
        acc_t = jnp.where(head0, pv[:span, :LANES], pv[span:, :LANES])
        l_t = jnp.where(head0, pv[:span, LANES:], pv[span:, LANES:])
        m_t = jnp.where(head0, m[:span], m[span:])
        if first:
            m_scr[rows, :] = m_t
            l_scr[rows, :] = l_t
            acc_scr[rows, :] = acc_t
        else:
            m_o = m_scr[rows, :]
            m_n = jnp.maximum(m_o, m_t)
            ea = jnp.exp2(m_o - m_n)
            eb = jnp.exp2(m_t - m_n)
            l_scr[rows, :] = ea * l_scr[rows, :] + eb * l_t
            acc_scr[rows, :] = ea * acc_scr[rows, :] + eb * acc_t
            m_scr[rows, :] = m_n

    for pi, (window, d) in enumerate(reversed(C_PATTERNS)):
        n_blocks = (seq // d) // span
        first = pi == 0

        def no_prev(t, carry, d=d, first=first):
            tile(t, d=d, blk0=0, first=first)
            return carry

        def with_prev(t, carry, d=d, first=first):
            tile(t, d=d, blk0=1, first=first)
            return carry

        lax.fori_loop(0, d, no_prev, 0, unroll=_unroll_for(d))
        if n_blocks > 1:
            trips = d * (n_blocks - 1)
            lax.fori_loop(0, trips, with_prev, 0, unroll=_unroll_for(trips))

    o_ref[...] = (acc_scr[...] * (1.0 / l_scr[...])).astype(BF16)


def _dilated(q, k, v):
    batch, seq, d = q.shape
    n_pairs = d // LANES
    blk = lambda: pl.BlockSpec((None, seq, LANES), lambda b, hp: (b, 0, hp))
    return pl.pallas_call(
        _dil_kernel,
        grid=(batch, n_pairs),
        in_specs=[blk(), blk(), blk()],
        out_specs=blk(),
        out_shape=jax.ShapeDtypeStruct((batch, seq, d), BF16),
        scratch_shapes=[pltpu.VMEM((seq, LANES), F32)] * 3 + [pltpu.VMEM((2 * DIL_SPAN, 2 * DIL_SPAN), F32)],
        compiler_params=_cparams(2),
        name="dilated",
    )(q, k, v)


def _rope_tables(seq):
    inv = 1.0 / (ROPE_THETA ** (jnp.arange(0, HEAD_DIM, 2, dtype=F32) / HEAD_DIM))
    ang = jnp.arange(seq, dtype=F32)[:, None] * inv[None, :]
    cos, sin = jnp.cos(ang), jnp.sin(ang)
    cos_t = jnp.concatenate([cos, cos, cos, cos], axis=1)
    sin_t = jnp.concatenate([-sin, sin, -sin, sin], axis=1)
    return cos_t, sin_t


def _pack_even_w_in(w):
    cuts = np.cumsum([0, 512, 64, 64, 512, 64, 8, 512, 512, 512])
    qa, ka, va, qi, ki, wi, qb, kb, vb = [w[:, cuts[i]:cuts[i + 1]] for i in range(9)]
    w_cols = jnp.concatenate([qa, qi, qb, kb, ka, ki, vb], axis=1).astype(BF16)
    w_t = jnp.concatenate([va, wi], axis=1).T
    pad = (-w_t.shape[0]) % BF16_ROWS
    w_t = jnp.pad(w_t, ((0, pad), (0, 0))).astype(BF16)
    assert w_cols.shape[1] == EVEN_ROPE_COLS + B_HEADS * B_VDIM and w_t.shape[0] >= EVEN_T_ROWS
    return w_cols, w_t


def kernel(x, p, norm_ffn_a, ffn_a_wg, ffn_a_wu, ffn_a_wd, norm_mix, norm_ffn_b, ffn_b_wg, ffn_b_wu,
           ffn_b_wd, norm_ple, ple_gate, ple_proj, even_w_in, even_w_out, diff_lambda_q1,
           diff_lambda_k1, diff_lambda_q2, diff_lambda_k2, diff_subln, odd_w_in, odd_w_out, final_norm):
    batch, seq, d_model = x.shape
    depth = p.shape[0]
    assert seq % ATTN_TQ == 0 and (batch * seq) % TOKEN_TILE == 0 and seq % TOKEN_TILE == 0
    assert seq == C_PATTERNS[-1][0], "dilated kernel assumes the widest window spans the sequence"
    top_k = min(A_TOPK_MAX, seq // 4)
    cos_t, sin_t = _rope_tables(seq)
    gain = lambda g: g.reshape(1, -1).astype(F32)
    n_buckets = seq // ATTN_TQ
    r3 = lambda a: a.reshape(batch, seq, a.shape[-1])
    p_all = p.reshape(depth, batch * seq, p.shape[-1])
    f32 = lambda w: w.astype(F32)
    ffn_a = (f32(ffn_a_wg), f32(ffn_a_wu), f32(ffn_a_wd))
    ffn_b = (f32(ffn_b_wg), f32(ffn_b_wu), f32(ffn_b_wd))

    h = x.reshape(batch * seq, d_model)
    for i in range(depth):
        ga, gm = gain(norm_ffn_a[i]), gain(norm_mix[i])
        if i % 2 == 0:
            e = i // 2
            lambda_init = 0.8 - 0.6 * math.exp(-0.3 * i)
            (ha, qa, qi, qb, kb, ka, ki, vb, va_t, wi_t) = _pre_even(
                h, ga, gm, *_pack_even_w_in(even_w_in[e]), cos_t, sin_t, *ffn_a, layer=i, batch=batch, seq=seq)
            ka, ki, vb = r3(ka), r3(ki), r3(vb)
            lam_vecs = jnp.stack([diff_lambda_q1[e], diff_lambda_k1[e],
                                  diff_lambda_q2[e], diff_lambda_k2[e]]).astype(F32)
            out_a = [_dsa_bucket(qa, qi, wi_t, ka, va_t, ki, j=j, top_k=top_k) for j in range(n_buckets)]
            out_b = _diff_attention(lam_vecs, gain(diff_subln[e]), qb, kb, vb, lambda_init=lambda_init)
            mixes = [jnp.concatenate(out_a, axis=2), out_b.reshape(batch * seq, -1)]
            mix_transposed = (True, False)
            w_out_all, mixer_layer = f32(even_w_out), e
        else:
            o = i // 2
            ha, q, k, v = _pre_odd(h, ga, gm, cos_t, sin_t, *ffn_a, f32(odd_w_in),
                                   layer=i, mixer_layer=o, seq=seq)
            mixes = [_dilated(r3(q), r3(k), r3(v)).reshape(batch * seq, -1)]
            mix_transposed = (False,)
            w_out_all, mixer_layer = f32(odd_w_out), o
        gf = gain(final_norm) if i == depth - 1 else None
        h = _post(ha, mixes, gain(norm_ffn_b[i]), gain(norm_ple[i]), p_all, gf,
                  w_out_all, *ffn_b, f32(ple_gate), f32(ple_proj),
                  mix_transposed=mix_transposed, layer=i, mixer_layer=mixer_layer, seq=seq)
    return h.reshape(batch, seq, d_model)
```

```python
import functools
import math

import jax
import jax.numpy as jnp
import numpy as np
from jax import lax
from jax.experimental import pallas as pl
from jax.experimental.pallas import tpu as pltpu

F32 = jnp.float32
BF16 = jnp.bfloat16
I32 = jnp.int32
I16 = jnp.int16

HEAD_DIM = 64
ROPE_THETA = 10000.0
NORM_EPS = 1e-6
SUBLN_EPS = 1e-5
NEG_INF = -1e30
A_HEADS = 8
A_TOPK_MAX = 256
IDX_HEADS = 8
B_HEADS = 4
B_VDIM = 2 * HEAD_DIM
C_PATTERNS = ((128, 1), (512, 4), (2048, 16))
DIL_SPAN = 128

LANES = 128
BF16_ROWS = 16
VMEM_LIMIT_BYTES = 56 * 1024 * 1024
TOKEN_TILE = 256
ATTN_TQ = 256
DIL_MAX_UNROLL = 16
WEIGHT_STAGE_ELEMS = 192 * 1024
WEIGHT_STAGE_SLOTS = 6
DSA_HEAD_GROUP = 4
DIFF_HEAD_GROUP = 4

ATTN_Q_SCALE = (HEAD_DIM ** -0.5) * math.log2(math.e)

INT_MIN = -(2 ** 31)
KEY_NEG_INF = INT_MIN + 0x7FFFFF
KEY_POS_INF = 0x7F800000
IDX_BIG = 2 ** 30

EVEN_ROPE_COLS = 4 * 512 + 2 * HEAD_DIM
EVEN_T_ROWS = HEAD_DIM + IDX_HEADS

_CONTRACT_LAST = (((1,), (1,)), ((), ()))
_CONTRACT_FIRST = (((0,), (0,)), ((), ()))


def _cparams(n_grid):
    return pltpu.CompilerParams(dimension_semantics=("arbitrary",) * n_grid,
                                vmem_limit_bytes=VMEM_LIMIT_BYTES)


def _resident(shape):
    nd = len(shape)
    return pl.BlockSpec(shape, lambda *_: (0,) * nd, pipeline_mode=pl.Buffered(1))


_HBM = pl.BlockSpec(memory_space=pl.ANY)


def _stage_rows(rows, cols):
    return max(r for r in range(BF16_ROWS, rows + 1, BF16_ROWS)
               if rows % r == 0 and r * cols <= WEIGHT_STAGE_ELEMS)


def _weight_scratch(shapes):
    ns = WEIGHT_STAGE_SLOTS
    stage = (ns, max(_stage_rows(r, c) for r, c in shapes), max(c for _, c in shapes))
    return [pltpu.VMEM(s, BF16) for s in shapes] + [pltpu.VMEM(stage, F32), pltpu.SemaphoreType.DMA((ns,))]


def _fetch_weights(layer, hbm_refs, vmem_refs, stage, sem):
    @pl.when(pl.program_id(0) == 0)
    def _():
        for w_hbm, w_vmem in zip(hbm_refs, vmem_refs):
            rows, cols = w_vmem.shape
            rc = _stage_rows(rows, cols)
            n = rows // rc
            ns = stage.shape[0]
            ahead = min(ns - 1, n)

            def copy(c, w_hbm=w_hbm, rc=rc, cols=cols):
                slot = c % ns
                return pltpu.make_async_copy(w_hbm.at[layer, pl.ds(c * rc, rc), :],
                                             stage.at[slot, pl.ds(0, rc), pl.ds(0, cols)], sem.at[slot])

            def body(c, carry, copy=copy, w_vmem=w_vmem, rc=rc, cols=cols, n=n, ahead=ahead):
                @pl.when(c + ahead < n)
                def _():
                    copy(c + ahead).start()
                copy(c).wait()
                w_vmem[pl.ds(pl.multiple_of(c * rc, rc), rc), :] = stage[c % ns, 0:rc, 0:cols].astype(BF16)
                return carry

            for c in range(ahead):
                copy(c).start()
            lax.fori_loop(0, n, body, 0)


def _rms(x, g, eps):
    ms = jnp.mean(x * x, axis=-1, keepdims=True)
    return x * lax.rsqrt(ms + eps) * g


def _ffn_half(h, g, wg_ref, wu_ref, wd_ref):
    xn = _rms(h, g, NORM_EPS).astype(BF16)
    a = jnp.dot(xn, wg_ref[...], preferred_element_type=F32)
    b = jnp.dot(xn, wu_ref[...], preferred_element_type=F32)
    mid = (a / (1.0 + jnp.exp(-a)) * b).astype(BF16)
    return h + 0.5 * jnp.dot(mid, wd_ref[...], preferred_element_type=F32)


def _rope(y, cos_t, sin_t):
    w = y.shape[1]
    n = w // LANES
    c = jnp.concatenate([cos_t] * n, axis=1) if n > 1 else cos_t
    s = jnp.concatenate([sin_t] * n, axis=1) if n > 1 else sin_t
    lane = lax.broadcasted_iota(I32, y.shape, 1)
    first_half = (lane & (HEAD_DIM // 2)) == 0
    partner = jnp.where(first_half,
                        pltpu.roll(y, w - HEAD_DIM // 2, 1),
                        pltpu.roll(y, HEAD_DIM // 2, 1))
    return y * c + partner * s


def _pre_even_kernel(h_ref, ga_ref, gm_ref, wr_ref, wt_ref, cos_ref, sin_ref, wg_hbm, wu_hbm, wd_hbm,
                     ha_ref, qa_ref, qi_ref, qb_ref, kb_ref, ka_ref, ki_ref, vb_ref, vat_ref, wit_ref,
                     wg_ref, wu_ref, wd_ref, stage, sem, *, layer, idx_scale):
    _fetch_weights(layer, (wg_hbm, wu_hbm, wd_hbm), (wg_ref, wu_ref, wd_ref), stage, sem)
    ha = _ffn_half(h_ref[...], ga_ref[...], wg_ref, wu_ref, wd_ref)
    ha_ref[...] = ha
    xn = _rms(ha, gm_ref[...], NORM_EPS).astype(BF16)
    y = jnp.dot(xn, wr_ref[...], preferred_element_type=F32)
    yr = _rope(y[:, :EVEN_ROPE_COLS], cos_ref[...], sin_ref[...])
    vb_ref[...] = y[:, EVEN_ROPE_COLS:EVEN_ROPE_COLS + B_HEADS * B_VDIM].astype(BF16)
    for hd in range(8):
        lo = hd * HEAD_DIM
        qa_ref[hd] = (yr[:, lo:lo + HEAD_DIM] * ATTN_Q_SCALE).astype(BF16)
        qi_ref[hd] = yr[:, 512 + lo:512 + lo + HEAD_DIM].astype(BF16)
        qb_ref[hd] = (yr[:, 1024 + lo:1024 + lo + HEAD_DIM] * ATTN_Q_SCALE).astype(BF16)
        kb_ref[hd] = yr[:, 1536 + lo:1536 + lo + HEAD_DIM].astype(BF16)
    ka_ref[...] = yr[:, 2048:2112].astype(BF16)
    ki_ref[...] = yr[:, 2112:2176].astype(BF16)
    yt = lax.dot_general(wt_ref[...], xn, _CONTRACT_LAST, preferred_element_type=F32)
    vat_ref[...] = yt[0:HEAD_DIM].astype(BF16)
    wit_ref[...] = yt[HEAD_DIM:HEAD_DIM + IDX_HEADS] * idx_scale


def _pre_odd_kernel(h_ref, ga_ref, gm_ref, cos_ref, sin_ref, wg_hbm, wu_hbm, wd_hbm, win_hbm,
                    ha_ref, q_ref, k_ref, v_ref,
                    wg_ref, wu_ref, wd_ref, win_ref, stage, sem, *, layer, mixer_layer):
    _fetch_weights(layer, (wg_hbm, wu_hbm, wd_hbm), (wg_ref, wu_ref, wd_ref), stage, sem)
    _fetch_weights(mixer_layer, (win_hbm,), (win_ref,), stage, sem)
    ha = _ffn_half(h_ref[...], ga_ref[...], wg_ref, wu_ref, wd_ref)
    ha_ref[...] = ha
    xn = _rms(ha, gm_ref[...], NORM_EPS).astype(BF16)
    y = jnp.dot(xn, win_ref[...], preferred_element_type=F32)
    d = q_ref.shape[1]
    yr = _rope(y[:, :2 * d], cos_ref[...], sin_ref[...])
    q_ref[...] = yr[:, :d] * ATTN_Q_SCALE
    k_ref[...] = yr[:, d:2 * d]
    v_ref[...] = y[:, 2 * d:3 * d]


def _token_specs(tm, seq_tiles):
    row = lambda w: pl.BlockSpec((tm, w), lambda i: (i, 0))
    rope = pl.BlockSpec((tm, LANES), lambda i: (i % seq_tiles, 0))
    return row, rope


def _pre_even(h, ga, gm, w_cols, w_t, cos_t, sin_t, wg_all, wu_all, wd_all, *, layer, batch, seq):
    m, d = h.shape
    tm = TOKEN_TILE
    nt = seq // tm
    row, rope = _token_specs(tm, nt)
    hm = lambda: pl.BlockSpec((None, 8, tm, HEAD_DIM), lambda i: (i // nt, 0, i % nt, 0))
    hm_shape = jax.ShapeDtypeStruct((batch, 8, seq, HEAD_DIM), BF16)
    tr = lambda rows: pl.BlockSpec((None, rows, tm), lambda i: (i // nt, 0, i % nt))
    idx_scale = (HEAD_DIM ** -0.5) * (IDX_HEADS ** -0.5)
    return pl.pallas_call(
        functools.partial(_pre_even_kernel, layer=layer, idx_scale=idx_scale),
        grid=(m // tm,),
        in_specs=[row(d), _resident(ga.shape), _resident(gm.shape), _resident(w_cols.shape),
                  _resident(w_t.shape), rope, rope, _HBM, _HBM, _HBM],
        out_specs=[row(d), hm(), hm(), hm(), hm(), row(HEAD_DIM), row(HEAD_DIM),
                   row(B_HEADS * B_VDIM), tr(HEAD_DIM), tr(IDX_HEADS)],
        out_shape=[jax.ShapeDtypeStruct((m, d), F32), hm_shape, hm_shape, hm_shape, hm_shape,
                   jax.ShapeDtypeStruct((m, HEAD_DIM), BF16), jax.ShapeDtypeStruct((m, HEAD_DIM), BF16),
                   jax.ShapeDtypeStruct((m, B_HEADS * B_VDIM), BF16),
                   jax.ShapeDtypeStruct((batch, HEAD_DIM, seq), BF16),
                   jax.ShapeDtypeStruct((batch, IDX_HEADS, seq), F32)],
        scratch_shapes=_weight_scratch([wg_all.shape[1:], wu_all.shape[1:], wd_all.shape[1:]]),
        compiler_params=_cparams(1),
        name="pre_even",
    )(h, ga, gm, w_cols, w_t, cos_t, sin_t, wg_all, wu_all, wd_all)


def _pre_odd(h, ga, gm, cos_t, sin_t, wg_all, wu_all, wd_all, win_all, *, layer, mixer_layer, seq):
    m, d = h.shape
    tm = TOKEN_TILE
    nt = seq // tm
    row, rope = _token_specs(tm, nt)
    out = jax.ShapeDtypeStruct((m, d), F32)
    return pl.pallas_call(
        functools.partial(_pre_odd_kernel, layer=layer, mixer_layer=mixer_layer),
        grid=(m // tm,),
        in_specs=[row(d), _resident(ga.shape), _resident(gm.shape), rope, rope, _HBM, _HBM, _HBM, _HBM],
        out_specs=[row(d), row(d), row(d), row(d)],
        out_shape=[out, out, out, out],
        scratch_shapes=_weight_scratch([wg_all.shape[1:], wu_all.shape[1:], wd_all.shape[1:], win_all.shape[1:]]),
        compiler_params=_cparams(1),
        name="pre_odd",
    )(h, ga, gm, cos_t, sin_t, wg_all, wu_all, wd_all, win_all)


def _post_kernel(*refs, n_mix, mix_transposed, final, layer, mixer_layer):
    refs = list(refs)
    h_ref = refs.pop(0)
    mix_refs = [refs.pop(0) for _ in range(n_mix)]
    gb_ref, gp_ref, p_ref = (refs.pop(0) for _ in range(3))
    gf_ref = refs.pop(0) if final else None
    wout_hbm, wg_hbm, wu_hbm, wd_hbm, pg_hbm, pp_hbm = (refs.pop(0) for _ in range(6))
    o_ref = refs.pop(0)
    wout_ref, wg_ref, wu_ref, wd_ref, pg_ref, pp_ref, stage, sem = refs
    _fetch_weights(mixer_layer, (wout_hbm,), (wout_ref,), stage, sem)
    _fetch_weights(layer, (wg_hbm, wu_hbm, wd_hbm, pg_hbm, pp_hbm), (wg_ref, wu_ref, wd_ref, pg_ref, pp_ref),
                   stage, sem)
    h1 = h_ref[...]
    row0 = 0
    for r, transposed in zip(mix_refs, mix_transposed):
        if transposed:
            w = wout_ref[row0:row0 + r.shape[0], :]
            h1 = h1 + lax.dot_general(r[...], w, _CONTRACT_FIRST, preferred_element_type=F32)
            row0 += r.shape[0]
        else:
            w = wout_ref[row0:row0 + r.shape[1], :]
            h1 = h1 + jnp.dot(r[...], w, preferred_element_type=F32)
            row0 += r.shape[1]
    h2 = _ffn_half(h1, gb_ref[...], wg_ref, wu_ref, wd_ref)
    z = jnp.dot(_rms(h2, gp_ref[...], NORM_EPS).astype(BF16), pg_ref[...], preferred_element_type=F32)
    gate = 1.0 / (1.0 + jnp.exp(-z))
    emb = jnp.dot(p_ref[...].astype(BF16), pp_ref[...], preferred_element_type=F32)
    h3 = h2 + gate * emb
    if final:
        h3 = _rms(h3, gf_ref[...], NORM_EPS)
    o_ref[...] = h3


def _post(h, mixes, gb, gp, p_all, gf, wout_all, wg_all, wu_all, wd_all, pg_all, pp_all,
          *, mix_transposed, layer, mixer_layer, seq):
    m, d = h.shape
    tm = TOKEN_TILE
    nt = seq // tm
    row = lambda w: pl.BlockSpec((tm, w), lambda i: (i, 0))
    mix_specs = [pl.BlockSpec((None, x.shape[1], tm), lambda i: (i // nt, 0, i % nt)) if t else row(x.shape[1])
                 for x, t in zip(mixes, mix_transposed)]
    final = gf is not None
    weights = [wout_all, wg_all, wu_all, wd_all, pg_all, pp_all]
    args = [h, *mixes, gb, gp, p_all] + ([gf] if final else []) + weights
    specs = ([row(d), *mix_specs, _resident(gb.shape), _resident(gp.shape),
              pl.BlockSpec((None, tm, p_all.shape[2]), lambda i: (layer, i, 0))]
             + ([_resident(gf.shape)] if final else []) + [_HBM] * len(weights))
    return pl.pallas_call(
        functools.partial(_post_kernel, n_mix=len(mixes), mix_transposed=mix_transposed, final=final,
                          layer=layer, mixer_layer=mixer_layer),
        grid=(m // tm,),
        in_specs=specs,
        out_specs=row(d),
        out_shape=jax.ShapeDtypeStruct((m, d), F32),
        scratch_shapes=_weight_scratch([w.shape[1:] for w in weights]),
        compiler_params=_cparams(1),
        name="post",
    )(*args)


def _key_count(mask):
    return jnp.sum(jnp.where(mask, 1.0, 0.0), axis=0, keepdims=True)


def _key_to_float(key):
    key = jnp.clip(key, jnp.int32(KEY_NEG_INF), jnp.int32(KEY_POS_INF))
    return pltpu.bitcast(jnp.where(key < 0, key ^ jnp.int32(0x7FFFFFFF), key), F32)


def _count16(mask):
    s = mask.shape[0]
    ones = jnp.where(mask, jnp.ones((), I16), jnp.zeros((), I16))
    parts = [ones[i:i + BF16_ROWS] for i in range(0, s, BF16_ROWS)]
    while len(parts) > 1:
        nxt = [parts[i] + parts[i + 1] for i in range(0, len(parts) - 1, 2)]
        parts = nxt + ([parts[-1]] if len(parts) % 2 else [])
    return jnp.sum(parts[0].astype(F32), axis=0, keepdims=True)


def _kth_largest_fast(val_scr, hi_scr, lo_scr, *, top_k):
    tq = val_scr.shape[1]
    half_min = jnp.int32(-(2 ** 15))

    def keys():
        bits = pltpu.bitcast(val_scr[...], I32)
        return jnp.where(bits < 0, bits ^ jnp.int32(0x7FFFFFFF), bits)

    hi_scr[...] = lax.shift_right_arithmetic(keys(), jnp.int32(16)).astype(I16)

    def search(ref, need):
        def step(it, acc):
            cand = acc + lax.shift_left(jnp.int32(1), jnp.int32(15) - it)
            cnt = _count16(ref[...] >= cand.astype(I16))
            return jnp.where(cnt >= need, cand, acc)
        return lax.fori_loop(0, 16, step, jnp.full((1, tq), half_min, I32))

    hi = search(hi_scr, top_k)
    hi16 = hi.astype(I16)
    need = top_k - _count16(hi_scr[...] > hi16)
    low = ((keys() & jnp.int32(0xFFFF)) + half_min).astype(I16)
    lo_scr[...] = jnp.where(hi_scr[...] == hi16, low, half_min.astype(I16))
    lo = search(lo_scr, need)
    return _key_to_float(lax.shift_left(hi, jnp.int32(16)) | ((lo - half_min) & jnp.int32(0xFFFF)))


def _kth_largest_float(val_scr, *, top_k):
    tq = val_scr.shape[1]

    def value_step(it, lo):
        cand = lo ^ lax.shift_left(jnp.int32(1), jnp.int32(31) - it)
        cnt = _key_count(val_scr[...] >= _key_to_float(cand))
        return jnp.where(cnt >= top_k, cand, lo)

    return _key_to_float(lax.fori_loop(0, 32, value_step, jnp.full((1, tq), INT_MIN, I32)))


def _topk_bias(score, causal, val_scr, idx_scr, hi_scr, lo_scr, thr_scr, *, top_k):
    s, tq = score.shape
    key_idx = lax.broadcasted_iota(I32, (s, tq), 0)
    val_scr[...] = jnp.where(causal, score, -jnp.inf)

    thr_scr[...] = jnp.broadcast_to(_kth_largest_fast(val_scr, hi_scr, lo_scr, top_k=top_k), thr_scr.shape)
    thr0 = thr_scr[0:1, :]
    verified = jnp.logical_and(jnp.min(_key_count(val_scr[...] >= thr0)) >= top_k,
                               jnp.max(_key_count(val_scr[...] > thr0)) < top_k)

    @pl.when(jnp.logical_not(verified))
    def _():
        thr_scr[...] = jnp.broadcast_to(_kth_largest_float(val_scr, top_k=top_k), thr_scr.shape)

    thr = thr_scr[0:1, :]
    n_ge = _key_count(val_scr[...] >= thr)
    excess = jnp.max(n_ge) > top_k

    @pl.when(jnp.logical_not(excess))
    def _():
        val_scr[...] = jnp.where(val_scr[...] >= thr, 0.0, NEG_INF)

    @pl.when(excess)
    def _():
        vals = val_scr[...]
        above = vals > thr
        need = top_k - _key_count(above)
        idx_scr[...] = jnp.where(vals == thr, key_idx, jnp.int32(IDX_BIG))
        n_bits = max(1, int(math.ceil(math.log2(s))))

        def index_step(it, p):
            cand = p + lax.shift_left(jnp.int32(1), jnp.int32(n_bits - 1) - it)
            cnt = _key_count(idx_scr[...] < cand)
            return jnp.where(cnt < need, cand, p)

        p = lax.fori_loop(0, n_bits, index_step, jnp.zeros((1, tq), I32))
        keep_tie = idx_scr[...] <= p
        keep = jnp.where(above, 0.0, jnp.where(keep_tie, 0.0, NEG_INF))
        val_scr[...] = jnp.where(causal, keep, NEG_INF)

    return val_scr[...]


def _with_ones_rows(v_t):
    return jnp.concatenate([v_t, jnp.ones((BF16_ROWS, v_t.shape[1]), BF16)], axis=0)


def _dsa_kernel(qa_ref, qi_ref, wit_ref, ka_ref, vat_ref, ki_ref, o_ref, *scratch, s_len, q_start, top_k):
    tq = o_ref.shape[1]
    key_idx = lax.broadcasted_iota(I32, (s_len, tq), 0)
    q_pos = lax.broadcasted_iota(I32, (s_len, tq), 1) + q_start
    causal = key_idx <= q_pos
    if s_len > top_k:
        ki = ki_ref[...]
        w_t = wit_ref[...]
        score = jnp.zeros((s_len, tq), F32)
        g = DSA_HEAD_GROUP
        for h0 in range(0, IDX_HEADS, g):
            dots = lax.dot_general(ki, qi_ref[h0:h0 + g].reshape(g * tq, HEAD_DIM), _CONTRACT_LAST,
                                   preferred_element_type=F32)
            for i in range(g):
                score = score + w_t[h0 + i:h0 + i + 1, :] * jnp.maximum(dots[:, i * tq:(i + 1) * tq], 0.0)
        bias = _topk_bias(score, causal, *scratch, top_k=top_k)
    else:
        bias = jnp.where(causal, 0.0, NEG_INF)
    ka = ka_ref[...]
    v_ext = _with_ones_rows(vat_ref[...])
    g = DSA_HEAD_GROUP
    bias_g = jnp.concatenate([bias] * g, axis=1)
    for h0 in range(0, A_HEADS, g):
        q_g = qa_ref[h0:h0 + g].reshape(g * tq, HEAD_DIM)
        s_t = lax.dot_general(ka, q_g, _CONTRACT_LAST, preferred_element_type=F32) + bias_g
        m = jnp.max(s_t, axis=0, keepdims=True)
        p = jnp.exp2(s_t - m).astype(BF16)
        r = jnp.dot(v_ext, p, preferred_element_type=F32)
        o = (r[0:HEAD_DIM] * (1.0 / r[HEAD_DIM:HEAD_DIM + 1])).astype(BF16)
        for i in range(g):
            o_ref[(h0 + i) * HEAD_DIM:(h0 + i + 1) * HEAD_DIM, :] = o[:, i * tq:(i + 1) * tq]


def _dsa_bucket(qa, qi, wi_t, ka, va_t, ki, *, j, top_k):
    batch, _, seq, _ = qa.shape
    tq = ATTN_TQ
    s_len = (j + 1) * tq
    hm = pl.BlockSpec((None, 8, tq, HEAD_DIM), lambda b: (b, 0, j, 0))
    kv = pl.BlockSpec((None, s_len, HEAD_DIM), lambda b: (b, 0, 0))
    scratch = [pltpu.VMEM((s_len, tq), F32), pltpu.VMEM((s_len, tq), I32), pltpu.VMEM((s_len, tq), I16),
               pltpu.VMEM((s_len, tq), I16), pltpu.VMEM((8, tq), F32)] if s_len > top_k else []
    return pl.pallas_call(
        functools.partial(_dsa_kernel, s_len=s_len, q_start=j * tq, top_k=top_k),
        grid=(batch,),
        in_specs=[hm, hm, pl.BlockSpec((None, IDX_HEADS, tq), lambda b: (b, 0, j)), kv,
                  pl.BlockSpec((None, HEAD_DIM, s_len), lambda b: (b, 0, 0)), kv],
        out_specs=pl.BlockSpec((None, A_HEADS * HEAD_DIM, tq), lambda b: (b, 0, 0)),
        out_shape=jax.ShapeDtypeStruct((batch, A_HEADS * HEAD_DIM, tq), BF16),
        scratch_shapes=scratch,
        compiler_params=_cparams(1),
        name=f"dsa_{j}",
    )(qa, qi, wi_t, ka, va_t, ki)


def _causal_attend(q, k_ref, v_ext, c, j, tri_bias):
    tq = ATTN_TQ
    off = j * tq
    s_d = lax.dot_general(q, k_ref[c, off:off + tq, :], _CONTRACT_LAST, preferred_element_type=F32) + tri_bias
    m = jnp.max(s_d, axis=1, keepdims=True)
    if j > 0:
        s_o = lax.dot_general(q, k_ref[c, 0:off, :], _CONTRACT_LAST, preferred_element_type=F32)
        m = jnp.maximum(m, jnp.max(s_o, axis=1, keepdims=True))
    r = jnp.dot(jnp.exp2(s_d - m).astype(BF16), v_ext[off:off + tq, :], preferred_element_type=F32)
    if j > 0:
        r = r + jnp.dot(jnp.exp2(s_o - m).astype(BF16), v_ext[0:off, :], preferred_element_type=F32)
    return r[:, :B_VDIM] * (1.0 / r[:, B_VDIM:])


def _diff_kernel(lam_ref, subln_ref, q_ref, k_ref, v_ref, o_ref, *, lambda_init):
    seq = o_ref.shape[0]
    tq = ATTN_TQ
    lv = lam_ref[...]
    lam = (jnp.exp(jnp.sum(lv[0:1] * lv[1:2], axis=1, keepdims=True))
           - jnp.exp(jnp.sum(lv[2:3] * lv[3:4], axis=1, keepdims=True)) + lambda_init)
    row = lax.broadcasted_iota(I32, (tq, tq), 0)
    col = lax.broadcasted_iota(I32, (tq, tq), 1)
    tri_bias = jnp.where(col <= row, 0.0, NEG_INF)
    subln = subln_ref[...]
    n_heads = q_ref.shape[0] // 2
    v_exts = []
    for hd in range(n_heads):
        v = v_ref[:, hd * B_VDIM:(hd + 1) * B_VDIM]
        v_exts.append(jnp.concatenate([v, jnp.ones(v.shape, BF16)], axis=1))
    for j in range(seq // tq):
        rows = slice(j * tq, (j + 1) * tq)
        for hd in range(n_heads):
            a = (_causal_attend(q_ref[2 * hd, rows, :], k_ref, v_exts[hd], 2 * hd, j, tri_bias)
                 - lam * _causal_attend(q_ref[2 * hd + 1, rows, :], k_ref, v_exts[hd], 2 * hd + 1, j, tri_bias))
            y = _rms(a, subln, SUBLN_EPS) * (1.0 - lambda_init)
            o_ref[rows, hd * B_VDIM:(hd + 1) * B_VDIM] = y.astype(BF16)


def _diff_attention(lam_vecs, subln, qb, kb, vb, *, lambda_init):
    batch, _, seq, _ = qb.shape
    hg = DIFF_HEAD_GROUP
    qk = lambda: pl.BlockSpec((None, 2 * hg, seq, HEAD_DIM), lambda b, h: (b, h, 0, 0))
    vo = lambda: pl.BlockSpec((None, seq, hg * B_VDIM), lambda b, h: (b, 0, h))
    return pl.pallas_call(
        functools.partial(_diff_kernel, lambda_init=lambda_init),
        grid=(batch, B_HEADS // hg),
        in_specs=[pl.BlockSpec(lam_vecs.shape, lambda b, h: (0, 0)),
                  pl.BlockSpec(subln.shape, lambda b, h: (0, 0)),
                  qk(), qk(), vo()],
        out_specs=vo(),
        out_shape=jax.ShapeDtypeStruct((batch, seq, B_HEADS * B_VDIM), BF16),
        compiler_params=_cparams(2),
        name="diff",
    )(lam_vecs, subln, qb, kb, vb)


def _unroll_for(trips):
    return max(u for u in range(1, DIL_MAX_UNROLL + 1) if trips % u == 0)


def _dil_kernel(q_ref, k_ref, v_ref, o_ref, m_scr, l_scr, acc_scr, bias_scr):
    seq = q_ref.shape[0]
    span = DIL_SPAN
    lane = lax.broadcasted_iota(I32, (span, LANES), 1)
    head0 = lane < HEAD_DIM

    qi = lax.broadcasted_iota(I32, (2 * span, 2 * span), 0) & (span - 1)
    kj = lax.broadcasted_iota(I32, (2 * span, 2 * span), 1)
    dist = kj - qi
    bias_scr[...] = jnp.where(jnp.where(kj < span, dist, span - dist) >= 0, 0.0, NEG_INF)

    def rows_at(start, d):
        return pl.ds(start, span) if d == 1 else pl.ds(start, span, stride=d)

    def tile(t, *, d, blk0, first):
        r = t % d
        blk = blk0 + t // d
        start = r + blk * (span * d)
        if d == 1:
            start = pl.multiple_of(start, span)
        rows = rows_at(start, d)
        qb = q_ref[rows, :]
        qs = jnp.concatenate([jnp.where(head0, qb, 0.0), jnp.where(head0, 0.0, qb)], axis=0).astype(BF16)
        kk = k_ref[rows, :]
        vv = v_ref[rows, :]
        has_prev = blk0 > 0
        if has_prev:
            prev = rows_at(start - span * d, d)
            kk = jnp.concatenate([k_ref[prev, :], kk], axis=0)
            vv = jnp.concatenate([v_ref[prev, :], vv], axis=0)
            bias = bias_scr[...]
        else:
            bias = bias_scr[:, span:]
        s = lax.dot_general(qs, kk.astype(BF16), _CONTRACT_LAST, preferred_element_type=F32) + bias
        m = jnp.max(s, axis=1, keepdims=True)
        p = jnp.exp2(s - m).astype(BF16)
        v_ext = jnp.concatenate([vv.astype(BF16), jnp.ones(vv.shape, BF16)], axis=1)
        pv = jnp.dot(p, v_ext, preferred_element_type=F32)
        acc_t = jnp.where(head0, pv[:span, :LANES], pv[span:, :LANES])
        l_t = jnp.where(head0, pv[:span, LANES:], pv[span:, LANES:])
        m_t = jnp.where(head0, m[:span], m[span:])
        if first:
            m_scr[rows, :] = m_t
            l_scr[rows, :] = l_t
            acc_scr[rows, :] = acc_t
        else:
            m_o = m_scr[rows, :]
            m_n = jnp.maximum(m_o, m_t)
            ea = jnp.exp2(m_o - m_n)
            eb = jnp.exp2(m_t - m_n)
            l_scr[rows, :] = ea * l_scr[rows, :] + eb * l_t
            acc_scr[rows, :] = ea * acc_scr[rows, :] + eb * acc_t
            m_scr[rows, :] = m_n

    for pi, (window, d) in enumerate(reversed(C_PATTERNS)):
        n_blocks = (seq // d) // span
        first = pi == 0

        def no_prev(t, carry, d=d, first=first):
            tile(t, d=d, blk0=0, first=first)
            return carry

        def with_prev(t, carry, d=d, first=first):
            tile(t, d=d, blk0=1, first=first)
            return carry

        lax.fori_loop(0, d, no_prev, 0, unroll=_unroll_for(d))
        if n_blocks > 1:
            trips = d * (n_blocks - 1)
            lax.fori_loop(0, trips, with_prev, 0, unroll=_unroll_for(trips))

    o_ref[...] = (acc_scr[...] * (1.0 / l_scr[...])).astype(BF16)


def _dilated(q, k, v):
    batch, seq, d = q.shape
    n_pairs = d // LANES
    blk = lambda: pl.BlockSpec((None, seq, LANES), lambda b, hp: (b, 0, hp))
    return pl.pallas_call(
        _dil_kernel,
        grid=(batch, n_pairs),
        in_specs=[blk(), blk(), blk()],
        out_specs=blk(),
        out_shape=jax.ShapeDtypeStruct((batch, seq, d), BF16),
        scratch_shapes=[pltpu.VMEM((seq, LANES), F32)] * 3 + [pltpu.VMEM((2 * DIL_SPAN, 2 * DIL_SPAN), F32)],
        compiler_params=_cparams(2),
        name="dilated",
    )(q, k, v)


def _rope_tables(seq):
    inv = 1.0 / (ROPE_THETA ** (jnp.arange(0, HEAD_DIM, 2, dtype=F32) / HEAD_DIM))
    ang = jnp.arange(seq, dtype=F32)[:, None] * inv[None, :]
    cos, sin = jnp.cos(ang), jnp.sin(ang)
    cos_t = jnp.concatenate([cos, cos, cos, cos], axis=1)
    sin_t = jnp.concatenate([-sin, sin, -sin, sin], axis=1)
    return cos_t, sin_t


def _pack_even_w_in(w):
    cuts = np.cumsum([0, 512, 64, 64, 512, 64, 8, 512, 512, 512])
    qa, ka, va, qi, ki, wi, qb, kb, vb = [w[:, cuts[i]:cuts[i + 1]] for i in range(9)]
    w_cols = jnp.concatenate([qa, qi, qb, kb, ka, ki, vb], axis=1).astype(BF16)
    w_t = jnp.concatenate([va, wi], axis=1).T
    pad = (-w_t.shape[0]) % BF16_ROWS
    w_t = jnp.pad(w_t, ((0, pad), (0, 0))).astype(BF16)
    assert w_cols.shape[1] == EVEN_ROPE_COLS + B_HEADS * B_VDIM and w_t.shape[0] >= EVEN_T_ROWS
    return w_cols, w_t


def kernel(x, p, norm_ffn_a, ffn_a_wg, ffn_a_wu, ffn_a_wd, norm_mix, norm_ffn_b, ffn_b_wg, ffn_b_wu,
           ffn_b_wd, norm_ple, ple_gate, ple_proj, even_w_in, even_w_out, diff_lambda_q1,
           diff_lambda_k1, diff_lambda_q2, diff_lambda_k2, diff_subln, odd_w_in, odd_w_out, final_norm):
    batch, seq, d_model = x.shape
    depth = p.shape[0]
    assert seq % ATTN_TQ == 0 and (batch * seq) % TOKEN_TILE == 0 and seq % TOKEN_TILE == 0
    assert seq == C_PATTERNS[-1][0], "dilated kernel assumes the widest window spans the sequence"
    top_k = min(A_TOPK_MAX, seq // 4)
    cos_t, sin_t = _rope_tables(seq)
    gain = lambda g: g.reshape(1, -1).astype(F32)
    n_buckets = seq // ATTN_TQ
    r3 = lambda a: a.reshape(batch, seq, a.shape[-1])
    p_all = p.reshape(depth, batch * seq, p.shape[-1])
    f32 = lambda w: w.astype(F32)
    ffn_a = (f32(ffn_a_wg), f32(ffn_a_wu), f32(ffn_a_wd))
    ffn_b = (f32(ffn_b_wg), f32(ffn_b_wu), f32(ffn_b_wd))

    h = x.reshape(batch * seq, d_model)
    for i in range(depth):
        ga, gm = gain(norm_ffn_a[i]), gain(norm_mix[i])
        if i % 2 == 0:
            e = i // 2
            lambda_init = 0.8 - 0.6 * math.exp(-0.3 * i)
            (ha, qa, qi, qb, kb, ka, ki, vb, va_t, wi_t) = _pre_even(
                h, ga, gm, *_pack_even_w_in(even_w_in[e]), cos_t, sin_t, *ffn_a, layer=i, batch=batch, seq=seq)
            ka, ki, vb = r3(ka), r3(ki), r3(vb)
            lam_vecs = jnp.stack([diff_lambda_q1[e], diff_lambda_k1[e],
                                  diff_lambda_q2[e], diff_lambda_k2[e]]).astype(F32)
            out_a = [_dsa_bucket(qa, qi, wi_t, ka, va_t, ki, j=j, top_k=top_k) for j in range(n_buckets)]
            out_b = _diff_attention(lam_vecs, gain(diff_subln[e]), qb, kb, vb, lambda_init=lambda_init)
            mixes = [jnp.concatenate(out_a, axis=2), out_b.reshape(batch * seq, -1)]
            mix_transposed = (True, False)
            w_out_all, mixer_layer = f32(even_w_out), e
        else:
            o = i // 2
            ha, q, k, v = _pre_odd(h, ga, gm, cos_t, sin_t, *ffn_a, f32(odd_w_in),
                                   layer=i, mixer_layer=o, seq=seq)
            mixes = [_dilated(r3(q), r3(k), r3(v)).reshape(batch * seq, -1)]
            mix_transposed = (False,)
            w_out_all, mixer_layer = f32(odd_w_out), o
        gf = gain(final_norm) if i == depth - 1 else None
        h = _post(ha, mixes, gain(norm_ffn_b[i]), gain(norm_ple[i]), p_all, gf,
                  w_out_all, *ffn_b, f32(ple_gate), f32(ple_proj),
                  mix_transposed=mix_transposed, layer=i, mixer_layer=mixer_layer, seq=seq)
    return h.reshape(batch, seq, d_model)
```

```python
import functools
import math

import jax
import jax.numpy as jnp
import numpy as np
from jax import lax
from jax.experimental import pallas as pl
from jax.experimental.pallas import tpu as pltpu

F32 = jnp.float32
BF16 = jnp.bfloat16
I32 = jnp.int32
I16 = jnp.int16

HEAD_DIM = 64
ROPE_THETA = 10000.0
NORM_EPS = 1e-6
SUBLN_EPS = 1e-5
NEG_INF = -1e30
A_HEADS = 8
A_TOPK_MAX = 256
IDX_HEADS = 8
B_HEADS = 4
B_VDIM = 2 * HEAD_DIM
C_PATTERNS = ((128, 1), (512, 4), (2048, 16))
DIL_SPAN = 128

LANES = 128
BF16_ROWS = 16
VMEM_LIMIT_BYTES = 60000 * 1024
TOKEN_TILE = 512
ATTN_TQ = 256
DIL_MAX_UNROLL = 16
WEIGHT_STAGE_ROWS = 64
WEIGHT_STAGE_SLOTS = 8
DSA_HEAD_GROUP = 4
DIFF_HEAD_GROUP = 4

ATTN_Q_SCALE = (HEAD_DIM ** -0.5) * math.log2(math.e)

INT_MIN = -(2 ** 31)
KEY_NEG_INF = INT_MIN + 0x7FFFFF
KEY_POS_INF = 0x7F800000
IDX_BIG = 2 ** 30

EVEN_ROPE_COLS = 4 * 512 + 2 * HEAD_DIM
EVEN_T_ROWS = HEAD_DIM + IDX_HEADS

_CONTRACT_LAST = (((1,), (1,)), ((), ()))
_CONTRACT_FIRST = (((0,), (0,)), ((), ()))


def _cparams(n_grid):
    return pltpu.CompilerParams(dimension_semantics=("arbitrary",) * n_grid,
                                vmem_limit_bytes=VMEM_LIMIT_BYTES)


def _resident(shape):
    nd = len(shape)
    return pl.BlockSpec(shape, lambda *_: (0,) * nd, pipeline_mode=pl.Buffered(1))


_HBM = pl.BlockSpec(memory_space=pl.ANY)


def _weight_scratch(shapes):
    ns = WEIGHT_STAGE_SLOTS
    assert all(r % WEIGHT_STAGE_ROWS == 0 for r, _ in shapes)
    stage = (ns, WEIGHT_STAGE_ROWS, max(c for _, c in shapes))
    return [pltpu.VMEM(s, BF16) for s in shapes] + [pltpu.VMEM(stage, F32), pltpu.SemaphoreType.DMA((ns,))]


def _fetch_weights(layer, hbm_refs, vmem_refs, stage, sem):
    @pl.when(pl.program_id(0) == 0)
    def _():
        for w_hbm, w_vmem in zip(hbm_refs, vmem_refs):
            rows, cols = w_vmem.shape
            rc = WEIGHT_STAGE_ROWS
            n = rows // rc
            ns = stage.shape[0]
            ahead = min(ns - 1, n)

            def copy(c, w_hbm=w_hbm, rc=rc, cols=cols):
                slot = c % ns
                return pltpu.make_async_copy(w_hbm.at[layer, pl.ds(c * rc, rc), :],
                                             stage.at[slot, pl.ds(0, rc), pl.ds(0, cols)], sem.at[slot])

            def body(c, carry, copy=copy, w_vmem=w_vmem, rc=rc, cols=cols, n=n, ahead=ahead):
                @pl.when(c + ahead < n)
                def _():
                    copy(c + ahead).start()
                copy(c).wait()
                w_vmem[pl.ds(pl.multiple_of(c * rc, rc), rc), :] = stage[c % ns, 0:rc, 0:cols].astype(BF16)
                return carry

            for c in range(ahead):
                copy(c).start()
            lax.fori_loop(0, n, body, 0)


def _rms(x, g, eps):
    ms = jnp.mean(x * x, axis=-1, keepdims=True)
    return x * lax.rsqrt(ms + eps) * g


def _ffn_half(h, g, wg_ref, wu_ref, wd_ref):
    xn = _rms(h, g, NORM_EPS).astype(BF16)
    a = jnp.dot(xn, wg_ref[...], preferred_element_type=F32)
    b = jnp.dot(xn, wu_ref[...], preferred_element_type=F32)
    mid = (a / (1.0 + jnp.exp(-a)) * b).astype(BF16)
    return h + 0.5 * jnp.dot(mid, wd_ref[...], preferred_element_type=F32)


def _rope(y, cos_t, sin_t):
    w = y.shape[1]
    n = w // LANES
    c = jnp.concatenate([cos_t] * n, axis=1) if n > 1 else cos_t
    s = jnp.concatenate([sin_t] * n, axis=1) if n > 1 else sin_t
    lane = lax.broadcasted_iota(I32, y.shape, 1)
    first_half = (lane & (HEAD_DIM // 2)) == 0
    partner = jnp.where(first_half,
                        pltpu.roll(y, w - HEAD_DIM // 2, 1),
                        pltpu.roll(y, HEAD_DIM // 2, 1))
    return y * c + partner * s


def _pre_even_kernel(h_ref, ga_ref, gm_ref, wr_ref, wt_ref, cos_ref, sin_ref, wg_hbm, wu_hbm, wd_hbm,
                     ha_ref, qa_ref, qi_ref, qb_ref, kb_ref, ka_ref, ki_ref, vb_ref, vat_ref, wit_ref,
                     wg_ref, wu_ref, wd_ref, stage, sem, *, layer, idx_scale):
    _fetch_weights(layer, (wg_hbm, wu_hbm, wd_hbm), (wg_ref, wu_ref, wd_ref), stage, sem)
    ha = _ffn_half(h_ref[...], ga_ref[...], wg_ref, wu_ref, wd_ref)
    ha_ref[...] = ha
    xn = _rms(ha, gm_ref[...], NORM_EPS).astype(BF16)
    y = jnp.dot(xn, wr_ref[...], preferred_element_type=F32)
    yr = _rope(y[:, :EVEN_ROPE_COLS], cos_ref[...], sin_ref[...])
    vb_ref[...] = y[:, EVEN_ROPE_COLS:EVEN_ROPE_COLS + B_HEADS * B_VDIM].astype(BF16)
    for hd in range(8):
        lo = hd * HEAD_DIM
        qa_ref[hd] = (yr[:, lo:lo + HEAD_DIM] * ATTN_Q_SCALE).astype(BF16)
        qi_ref[hd] = yr[:, 512 + lo:512 + lo + HEAD_DIM].astype(BF16)
        qb_ref[hd] = (yr[:, 1024 + lo:1024 + lo + HEAD_DIM] * ATTN_Q_SCALE).astype(BF16)
        kb_ref[hd] = yr[:, 1536 + lo:1536 + lo + HEAD_DIM].astype(BF16)
    ka_ref[...] = yr[:, 2048:2112].astype(BF16)
    ki_ref[...] = yr[:, 2112:2176].astype(BF16)
    yt = lax.dot_general(wt_ref[...], xn, _CONTRACT_LAST, preferred_element_type=F32)
    vat_ref[...] = yt[0:HEAD_DIM].astype(BF16)
    wit_ref[...] = yt[HEAD_DIM:HEAD_DIM + IDX_HEADS] * idx_scale


def _pre_odd_kernel(h_ref, ga_ref, gm_ref, cos_ref, sin_ref, wg_hbm, wu_hbm, wd_hbm, win_hbm,
                    ha_ref, q_ref, k_ref, v_ref,
                    wg_ref, wu_ref, wd_ref, win_ref, stage, sem, *, layer, mixer_layer):
    _fetch_weights(layer, (wg_hbm, wu_hbm, wd_hbm), (wg_ref, wu_ref, wd_ref), stage, sem)
    _fetch_weights(mixer_layer, (win_hbm,), (win_ref,), stage, sem)
    ha = _ffn_half(h_ref[...], ga_ref[...], wg_ref, wu_ref, wd_ref)
    ha_ref[...] = ha
    xn = _rms(ha, gm_ref[...], NORM_EPS).astype(BF16)
    y = jnp.dot(xn, win_ref[...], preferred_element_type=F32)
    d = q_ref.shape[1]
    yr = _rope(y[:, :2 * d], cos_ref[...], sin_ref[...])
    q_ref[...] = yr[:, :d] * ATTN_Q_SCALE
    k_ref[...] = yr[:, d:2 * d]
    v_ref[...] = y[:, 2 * d:3 * d]


def _token_specs(tm, seq_tiles):
    row = lambda w: pl.BlockSpec((tm, w), lambda i: (i, 0))
    rope = pl.BlockSpec((tm, LANES), lambda i: (i % seq_tiles, 0))
    return row, rope


def _pre_even(h, ga, gm, w_cols, w_t, cos_t, sin_t, wg_all, wu_all, wd_all, *, layer, batch, seq):
    m, d = h.shape
    tm = TOKEN_TILE
    nt = seq // tm
    row, rope = _token_specs(tm, nt)
    hm = lambda: pl.BlockSpec((None, 8, tm, HEAD_DIM), lambda i: (i // nt, 0, i % nt, 0))
    hm_shape = jax.ShapeDtypeStruct((batch, 8, seq, HEAD_DIM), BF16)
    tr = lambda rows: pl.BlockSpec((None, rows, tm), lambda i: (i // nt, 0, i % nt))
    idx_scale = (HEAD_DIM ** -0.5) * (IDX_HEADS ** -0.5)
    return pl.pallas_call(
        functools.partial(_pre_even_kernel, layer=layer, idx_scale=idx_scale),
        grid=(m // tm,),
        in_specs=[row(d), _resident(ga.shape), _resident(gm.shape), _resident(w_cols.shape),
                  _resident(w_t.shape), rope, rope, _HBM, _HBM, _HBM],
        out_specs=[row(d), hm(), hm(), hm(), hm(), row(HEAD_DIM), row(HEAD_DIM),
                   row(B_HEADS * B_VDIM), tr(HEAD_DIM), tr(IDX_HEADS)],
        out_shape=[jax.ShapeDtypeStruct((m, d), F32), hm_shape, hm_shape, hm_shape, hm_shape,
                   jax.ShapeDtypeStruct((m, HEAD_DIM), BF16), jax.ShapeDtypeStruct((m, HEAD_DIM), BF16),
                   jax.ShapeDtypeStruct((m, B_HEADS * B_VDIM), BF16),
                   jax.ShapeDtypeStruct((batch, HEAD_DIM, seq), BF16),
                   jax.ShapeDtypeStruct((batch, IDX_HEADS, seq), F32)],
        scratch_shapes=_weight_scratch([wg_all.shape[1:], wu_all.shape[1:], wd_all.shape[1:]]),
        compiler_params=_cparams(1),
        name="pre_even",
    )(h, ga, gm, w_cols, w_t, cos_t, sin_t, wg_all, wu_all, wd_all)


def _pre_odd(h, ga, gm, cos_t, sin_t, wg_all, wu_all, wd_all, win_all, *, layer, mixer_layer, seq):
    m, d = h.shape
    tm = TOKEN_TILE
    nt = seq // tm
    row, rope = _token_specs(tm, nt)
    out = jax.ShapeDtypeStruct((m, d), F32)
    return pl.pallas_call(
        functools.partial(_pre_odd_kernel, layer=layer, mixer_layer=mixer_layer),
        grid=(m // tm,),
        in_specs=[row(d), _resident(ga.shape), _resident(gm.shape), rope, rope, _HBM, _HBM, _HBM, _HBM],
        out_specs=[row(d), row(d), row(d), row(d)],
        out_shape=[out, out, out, out],
        scratch_shapes=_weight_scratch([wg_all.shape[1:], wu_all.shape[1:], wd_all.shape[1:], win_all.shape[1:]]),
        compiler_params=_cparams(1),
        name="pre_odd",
    )(h, ga, gm, cos_t, sin_t, wg_all, wu_all, wd_all, win_all)


def _post_kernel(*refs, n_mix, mix_transposed, final, layer, mixer_layer):
    refs = list(refs)
    h_ref = refs.pop(0)
    mix_refs = [refs.pop(0) for _ in range(n_mix)]
    gb_ref, gp_ref, p_ref = (refs.pop(0) for _ in range(3))
    gf_ref = refs.pop(0) if final else None
    wout_hbm, wg_hbm, wu_hbm, wd_hbm, pg_hbm, pp_hbm = (refs.pop(0) for _ in range(6))
    o_ref = refs.pop(0)
    wout_ref, wg_ref, wu_ref, wd_ref, pg_ref, pp_ref, stage, sem = refs
    _fetch_weights(mixer_layer, (wout_hbm,), (wout_ref,), stage, sem)
    _fetch_weights(layer, (wg_hbm, wu_hbm, wd_hbm, pg_hbm, pp_hbm), (wg_ref, wu_ref, wd_ref, pg_ref, pp_ref),
                   stage, sem)
    h1 = h_ref[...]
    row0 = 0
    for r, transposed in zip(mix_refs, mix_transposed):
        if transposed:
            w = wout_ref[row0:row0 + r.shape[0], :]
            h1 = h1 + lax.dot_general(r[...], w, _CONTRACT_FIRST, preferred_element_type=F32)
            row0 += r.shape[0]
        else:
            w = wout_ref[row0:row0 + r.shape[1], :]
            h1 = h1 + jnp.dot(r[...], w, preferred_element_type=F32)
            row0 += r.shape[1]
    h2 = _ffn_half(h1, gb_ref[...], wg_ref, wu_ref, wd_ref)
    z = jnp.dot(_rms(h2, gp_ref[...], NORM_EPS).astype(BF16), pg_ref[...], preferred_element_type=F32)
    gate = 1.0 / (1.0 + jnp.exp(-z))
    emb = jnp.dot(p_ref[...].astype(BF16), pp_ref[...], preferred_element_type=F32)
    h3 = h2 + gate * emb
    if final:
        h3 = _rms(h3, gf_ref[...], NORM_EPS)
    o_ref[...] = h3


def _post(h, mixes, gb, gp, p_all, gf, wout_all, wg_all, wu_all, wd_all, pg_all, pp_all,
          *, mix_transposed, layer, mixer_layer, seq):
    m, d = h.shape
    tm = TOKEN_TILE
    nt = seq // tm
    row = lambda w: pl.BlockSpec((tm, w), lambda i: (i, 0))
    mix_specs = [pl.BlockSpec((None, x.shape[1], tm), lambda i: (i // nt, 0, i % nt)) if t else row(x.shape[1])
                 for x, t in zip(mixes, mix_transposed)]
    final = gf is not None
    weights = [wout_all, wg_all, wu_all, wd_all, pg_all, pp_all]
    args = [h, *mixes, gb, gp, p_all] + ([gf] if final else []) + weights
    specs = ([row(d), *mix_specs, _resident(gb.shape), _resident(gp.shape),
              pl.BlockSpec((None, tm, p_all.shape[2]), lambda i: (layer, i, 0))]
             + ([_resident(gf.shape)] if final else []) + [_HBM] * len(weights))
    return pl.pallas_call(
        functools.partial(_post_kernel, n_mix=len(mixes), mix_transposed=mix_transposed, final=final,
                          layer=layer, mixer_layer=mixer_layer),
        grid=(m // tm,),
        in_specs=specs,
        out_specs=row(d),
        out_shape=jax.ShapeDtypeStruct((m, d), F32),
        scratch_shapes=_weight_scratch([w.shape[1:] for w in weights]),
        compiler_params=_cparams(1),
        name="post",
    )(*args)


def _key_count(mask):
    return jnp.sum(jnp.where(mask, 1.0, 0.0), axis=0, keepdims=True)


def _key_to_float(key):
    key = jnp.clip(key, jnp.int32(KEY_NEG_INF), jnp.int32(KEY_POS_INF))
    return pltpu.bitcast(jnp.where(key < 0, key ^ jnp.int32(0x7FFFFFFF), key), F32)


def _count16(mask):
    s = mask.shape[0]
    ones = jnp.where(mask, jnp.ones((), I16), jnp.zeros((), I16))
    parts = [ones[i:i + BF16_ROWS] for i in range(0, s, BF16_ROWS)]
    while len(parts) > 1:
        nxt = [parts[i] + parts[i + 1] for i in range(0, len(parts) - 1, 2)]
        parts = nxt + ([parts[-1]] if len(parts) % 2 else [])
    return jnp.sum(parts[0].astype(F32), axis=0, keepdims=True)


def _kth_largest_fast(val_scr, hi_scr, lo_scr, *, top_k):
    tq = val_scr.shape[1]
    half_min = jnp.int32(-(2 ** 15))

    def keys():
        bits = pltpu.bitcast(val_scr[...], I32)
        return jnp.where(bits < 0, bits ^ jnp.int32(0x7FFFFFFF), bits)

    hi_scr[...] = lax.shift_right_arithmetic(keys(), jnp.int32(16)).astype(I16)

    def search(ref, need):
        def step(it, acc):
            cand = acc + lax.shift_left(jnp.int32(1), jnp.int32(15) - it)
            cnt = _count16(ref[...] >= cand.astype(I16))
            return jnp.where(cnt >= need, cand, acc)
        return lax.fori_loop(0, 16, step, jnp.full((1, tq), half_min, I32))

    hi = search(hi_scr, top_k)
    hi16 = hi.astype(I16)
    need = top_k - _count16(hi_scr[...] > hi16)
    low = ((keys() & jnp.int32(0xFFFF)) + half_min).astype(I16)
    lo_scr[...] = jnp.where(hi_scr[...] == hi16, low, half_min.astype(I16))
    lo = search(lo_scr, need)
    return _key_to_float(lax.shift_left(hi, jnp.int32(16)) | ((lo - half_min) & jnp.int32(0xFFFF)))


def _kth_largest_float(val_scr, *, top_k):
    tq = val_scr.shape[1]

    def value_step(it, lo):
        cand = lo ^ lax.shift_left(jnp.int32(1), jnp.int32(31) - it)
        cnt = _key_count(val_scr[...] >= _key_to_float(cand))
        return jnp.where(cnt >= top_k, cand, lo)

    return _key_to_float(lax.fori_loop(0, 32, value_step, jnp.full((1, tq), INT_MIN, I32)))


def _topk_bias(score, causal, val_scr, idx_scr, hi_scr, lo_scr, thr_scr, *, top_k):
    s, tq = score.shape
    key_idx = lax.broadcasted_iota(I32, (s, tq), 0)
    val_scr[...] = jnp.where(causal, score, -jnp.inf)

    thr_scr[...] = jnp.broadcast_to(_kth_largest_fast(val_scr, hi_scr, lo_scr, top_k=top_k), thr_scr.shape)
    thr0 = thr_scr[0:1, :]
    verified = jnp.logical_and(jnp.min(_key_count(val_scr[...] >= thr0)) >= top_k,
                               jnp.max(_key_count(val_scr[...] > thr0)) < top_k)

    @pl.when(jnp.logical_not(verified))
    def _():
        thr_scr[...] = jnp.broadcast_to(_kth_largest_float(val_scr, top_k=top_k), thr_scr.shape)

    thr = thr_scr[0:1, :]
    n_ge = _key_count(val_scr[...] >= thr)
    excess = jnp.max(n_ge) > top_k

    @pl.when(jnp.logical_not(excess))
    def _():
        val_scr[...] = jnp.where(val_scr[...] >= thr, 0.0, NEG_INF)

    @pl.when(excess)
    def _():
        vals = val_scr[...]
        above = vals > thr
        need = top_k - _key_count(above)
        idx_scr[...] = jnp.where(vals == thr, key_idx, jnp.int32(IDX_BIG))
        n_bits = max(1, int(math.ceil(math.log2(s))))

        def index_step(it, p):
            cand = p + lax.shift_left(jnp.int32(1), jnp.int32(n_bits - 1) - it)
            cnt = _key_count(idx_scr[...] < cand)
            return jnp.where(cnt < need, cand, p)

        p = lax.fori_loop(0, n_bits, index_step, jnp.zeros((1, tq), I32))
        keep_tie = idx_scr[...] <= p
        keep = jnp.where(above, 0.0, jnp.where(keep_tie, 0.0, NEG_INF))
        val_scr[...] = jnp.where(causal, keep, NEG_INF)

    return val_scr[...]


def _with_ones_rows(v_t):
    return jnp.concatenate([v_t, jnp.ones((BF16_ROWS, v_t.shape[1]), BF16)], axis=0)


def _dsa_kernel(qa_ref, qi_ref, wit_ref, ka_ref, vat_ref, ki_ref, o_ref, *scratch, s_len, q_start, top_k):
    tq = o_ref.shape[1]
    key_idx = lax.broadcasted_iota(I32, (s_len, tq), 0)
    q_pos = lax.broadcasted_iota(I32, (s_len, tq), 1) + q_start
    causal = key_idx <= q_pos
    if s_len > top_k:
        ki = ki_ref[...]
        w_t = wit_ref[...]
        score = jnp.zeros((s_len, tq), F32)
        g = DSA_HEAD_GROUP
        for h0 in range(0, IDX_HEADS, g):
            dots = lax.dot_general(ki, qi_ref[h0:h0 + g].reshape(g * tq, HEAD_DIM), _CONTRACT_LAST,
                                   preferred_element_type=F32)
            for i in range(g):
                score = score + w_t[h0 + i:h0 + i + 1, :] * jnp.maximum(dots[:, i * tq:(i + 1) * tq], 0.0)
        bias = _topk_bias(score, causal, *scratch, top_k=top_k)
    else:
        bias = jnp.where(causal, 0.0, NEG_INF)
    ka = ka_ref[...]
    v_ext = _with_ones_rows(vat_ref[...])
    g = DSA_HEAD_GROUP
    bias_g = jnp.concatenate([bias] * g, axis=1)
    for h0 in range(0, A_HEADS, g):
        q_g = qa_ref[h0:h0 + g].reshape(g * tq, HEAD_DIM)
        s_t = lax.dot_general(ka, q_g, _CONTRACT_LAST, preferred_element_type=F32) + bias_g
        m = jnp.max(s_t, axis=0, keepdims=True)
        p = jnp.exp2(s_t - m).astype(BF16)
        r = jnp.dot(v_ext, p, preferred_element_type=F32)
        o = (r[0:HEAD_DIM] * (1.0 / r[HEAD_DIM:HEAD_DIM + 1])).astype(BF16)
        for i in range(g):
            o_ref[(h0 + i) * HEAD_DIM:(h0 + i + 1) * HEAD_DIM, :] = o[:, i * tq:(i + 1) * tq]


def _dsa_bucket(qa, qi, wi_t, ka, va_t, ki, *, j, top_k):
    batch, _, seq, _ = qa.shape
    tq = ATTN_TQ
    s_len = (j + 1) * tq
    hm = pl.BlockSpec((None, 8, tq, HEAD_DIM), lambda b: (b, 0, j, 0))
    kv = pl.BlockSpec((None, s_len, HEAD_DIM), lambda b: (b, 0, 0))
    scratch = [pltpu.VMEM((s_len, tq), F32), pltpu.VMEM((s_len, tq), I32), pltpu.VMEM((s_len, tq), I16),
               pltpu.VMEM((s_len, tq), I16), pltpu.VMEM((8, tq), F32)] if s_len > top_k else []
    return pl.pallas_call(
        functools.partial(_dsa_kernel, s_len=s_len, q_start=j * tq, top_k=top_k),
        grid=(batch,),
        in_specs=[hm, hm, pl.BlockSpec((None, IDX_HEADS, tq), lambda b: (b, 0, j)), kv,
                  pl.BlockSpec((None, HEAD_DIM, s_len), lambda b: (b, 0, 0)), kv],
        out_specs=pl.BlockSpec((None, A_HEADS * HEAD_DIM, tq), lambda b: (b, 0, 0)),
        out_shape=jax.ShapeDtypeStruct((batch, A_HEADS * HEAD_DIM, tq), BF16),
        scratch_shapes=scratch,
        compiler_params=_cparams(1),
        name=f"dsa_{j}",
    )(qa, qi, wi_t, ka, va_t, ki)


def _causal_attend(q, k_ref, v_ext, c, j, tri_bias):
    tq = ATTN_TQ
    off = j * tq
    s_d = lax.dot_general(q, k_ref[c, off:off + tq, :], _CONTRACT_LAST, preferred_element_type=F32) + tri_bias
    m = jnp.max(s_d, axis=1, keepdims=True)
    if j > 0:
        s_o = lax.dot_general(q, k_ref[c, 0:off, :], _CONTRACT_LAST, preferred_element_type=F32)
        m = jnp.maximum(m, jnp.max(s_o, axis=1, keepdims=True))
    r = jnp.dot(jnp.exp2(s_d - m).astype(BF16), v_ext[off:off + tq, :], preferred_element_type=F32)
    if j > 0:
        r = r + jnp.dot(jnp.exp2(s_o - m).astype(BF16), v_ext[0:off, :], preferred_element_type=F32)
    return r[:, :B_VDIM] * (1.0 / r[:, B_VDIM:])


def _diff_kernel(lam_ref, subln_ref, q_ref, k_ref, v_ref, o_ref, *, lambda_init):
    seq = o_ref.shape[0]
    tq = ATTN_TQ
    lv = lam_ref[...]
    lam = (jnp.exp(jnp.sum(lv[0:1] * lv[1:2], axis=1, keepdims=True))
           - jnp.exp(jnp.sum(lv[2:3] * lv[3:4], axis=1, keepdims=True)) + lambda_init)
    row = lax.broadcasted_iota(I32, (tq, tq), 0)
    col = lax.broadcasted_iota(I32, (tq, tq), 1)
    tri_bias = jnp.where(col <= row, 0.0, NEG_INF)
    subln = subln_ref[...]
    n_heads = q_ref.shape[0] // 2
    v_exts = []
    for hd in range(n_heads):
        v = v_ref[:, hd * B_VDIM:(hd + 1) * B_VDIM]
        v_exts.append(jnp.concatenate([v, jnp.ones(v.shape, BF16)], axis=1))
    for j in range(seq // tq):
        rows = slice(j * tq, (j + 1) * tq)
        for hd in range(n_heads):
            a = (_causal_attend(q_ref[2 * hd, rows, :], k_ref, v_exts[hd], 2 * hd, j, tri_bias)
                 - lam * _causal_attend(q_ref[2 * hd + 1, rows, :], k_ref, v_exts[hd], 2 * hd + 1, j, tri_bias))
            y = _rms(a, subln, SUBLN_EPS) * (1.0 - lambda_init)
            o_ref[rows, hd * B_VDIM:(hd + 1) * B_VDIM] = y.astype(BF16)


def _diff_attention(lam_vecs, subln, qb, kb, vb, *, lambda_init):
    batch, _, seq, _ = qb.shape
    hg = DIFF_HEAD_GROUP
    qk = lambda: pl.BlockSpec((None, 2 * hg, seq, HEAD_DIM), lambda b, h: (b, h, 0, 0))
    vo = lambda: pl.BlockSpec((None, seq, hg * B_VDIM), lambda b, h: (b, 0, h))
    return pl.pallas_call(
        functools.partial(_diff_kernel, lambda_init=lambda_init),
        grid=(batch, B_HEADS // hg),
        in_specs=[pl.BlockSpec(lam_vecs.shape, lambda b, h: (0, 0)),
                  pl.BlockSpec(subln.shape, lambda b, h: (0, 0)),
                  qk(), qk(), vo()],
        out_specs=vo(),
        out_shape=jax.ShapeDtypeStruct((batch, seq, B_HEADS * B_VDIM), BF16),
        compiler_params=_cparams(2),
        name="diff",
    )(lam_vecs, subln, qb, kb, vb)


def _unroll_for(trips):
    return max(u for u in range(1, DIL_MAX_UNROLL + 1) if trips % u == 0)


def _dil_kernel(q_ref, k_ref, v_ref, o_ref, m_scr, l_scr, acc_scr, bias_scr):
    seq = q_ref.shape[0]
    span = DIL_SPAN
    lane = lax.broadcasted_iota(I32, (span, LANES), 1)
    head0 = lane < HEAD_DIM

    qi = lax.broadcasted_iota(I32, (2 * span, 2 * span), 0) & (span - 1)
    kj = lax.broadcasted_iota(I32, (2 * span, 2 * span), 1)
    dist = kj - qi
    bias_scr[...] = jnp.where(jnp.where(kj < span, dist, span - dist) >= 0, 0.0, NEG_INF)

    def rows_at(start, d):
        return pl.ds(start, span) if d == 1 else pl.ds(start, span, stride=d)

    def tile(t, *, d, blk0, first):
        r = t % d
        blk = blk0 + t // d
        start = r + blk * (span * d)
        if d == 1:
            start = pl.multiple_of(start, span)
        rows = rows_at(start, d)
        qb = q_ref[rows, :]
        qs = jnp.concatenate([jnp.where(head0, qb, 0.0), jnp.where(head0, 0.0, qb)], axis=0).astype(BF16)
        kk = k_ref[rows, :]
        vv = v_ref[rows, :]
        has_prev = blk0 > 0
        if has_prev:
            prev = rows_at(start - span * d, d)
            kk = jnp.concatenate([k_ref[prev, :], kk], axis=0)
            vv = jnp.concatenate([v_ref[prev, :], vv], axis=0)
            bias = bias_scr[...]
        else:
            bias = bias_scr[:, span:]
        s = lax.dot_general(qs, kk.astype(BF16), _CONTRACT_LAST, preferred_element_type=F32) + bias
        m = jnp.max(s, axis=1, keepdims=True)
        p = jnp.exp2(s - m).astype(BF16)
        v_ext = jnp.concatenate([vv.astype(BF16), jnp.ones(vv.shape, BF16)], axis=1)
        pv = jnp.dot(p, v_ext, preferred_element_type=F32)
        acc_t = jnp.where(head0, pv[:span, :LANES], pv[span:, :LANES])
        l_t = jnp.where(head0, pv[:span, LANES:], pv[span:, LANES:])
        m_t = jnp.where(head0, m[:span], m[span:])
        if first:
            m_scr[rows, :] = m_t
            l_scr[rows, :] = l_t
            acc_scr[rows, :] = acc_t
        else:
            m_o = m_scr[rows, :]
            m_n = jnp.maximum(m_o, m_t)
            ea = jnp.exp2(m_o - m_n)
            eb = jnp.exp2(m_t - m_n)
            l_scr[rows, :] = ea * l_scr[rows, :] + eb * l_t
            acc_scr[rows, :] = ea * acc_scr[rows, :] + eb * acc_t
            m_scr[rows, :] = m_n

    for pi, (window, d) in enumerate(reversed(C_PATTERNS)):
        n_blocks = (seq // d) // span
        first = pi == 0

        def no_prev(t, carry, d=d, first=first):
            tile(t, d=d, blk0=0, first=first)
            return carry

        def with_prev(t, carry, d=d, first=first):
            tile(t, d=d, blk0=1, first=first)
            return carry

        lax.fori_loop(0, d, no_prev, 0, unroll=_unroll_for(d))
        if n_blocks > 1:
            trips = d * (n_blocks - 1)
            lax.fori_loop(0, trips, with_prev, 0, unroll=_unroll_for(trips))

    o_ref[...] = (acc_scr[...] * (1.0 / l_scr[...])).astype(BF16)


def _dilated(q, k, v):
    batch, seq, d = q.shape
    n_pairs = d // LANES
    blk = lambda: pl.BlockSpec((None, seq, LANES), lambda b, hp: (b, 0, hp))
    return pl.pallas_call(
        _dil_kernel,
        grid=(batch, n_pairs),
        in_specs=[blk(), blk(), blk()],
        out_specs=blk(),
        out_shape=jax.ShapeDtypeStruct((batch, seq, d), BF16),
        scratch_shapes=[pltpu.VMEM((seq, LANES), F32)] * 3 + [pltpu.VMEM((2 * DIL_SPAN, 2 * DIL_SPAN), F32)],
        compiler_params=_cparams(2),
        name="dilated",
    )(q, k, v)


def _rope_tables(seq):
    inv = 1.0 / (ROPE_THETA ** (jnp.arange(0, HEAD_DIM, 2, dtype=F32) / HEAD_DIM))
    ang = jnp.arange(seq, dtype=F32)[:, None] * inv[None, :]
    cos, sin = jnp.cos(ang), jnp.sin(ang)
    cos_t = jnp.concatenate([cos, cos, cos, cos], axis=1)
    sin_t = jnp.concatenate([-sin, sin, -sin, sin], axis=1)
    return cos_t, sin_t


def _pack_even_w_in(w):
    cuts = np.cumsum([0, 512, 64, 64, 512, 64, 8, 512, 512, 512])
    qa, ka, va, qi, ki, wi, qb, kb, vb = [w[:, cuts[i]:cuts[i + 1]] for i in range(9)]
    w_cols = jnp.concatenate([qa, qi, qb, kb, ka, ki, vb], axis=1).astype(BF16)
    w_t = jnp.concatenate([va, wi], axis=1).T
    pad = (-w_t.shape[0]) % BF16_ROWS
    w_t = jnp.pad(w_t, ((0, pad), (0, 0))).astype(BF16)
    assert w_cols.shape[1] == EVEN_ROPE_COLS + B_HEADS * B_VDIM and w_t.shape[0] >= EVEN_T_ROWS
    return w_cols, w_t


def kernel(x, p, norm_ffn_a, ffn_a_wg, ffn_a_wu, ffn_a_wd, norm_mix, norm_ffn_b, ffn_b_wg, ffn_b_wu,
           ffn_b_wd, norm_ple, ple_gate, ple_proj, even_w_in, even_w_out, diff_lambda_q1,
           diff_lambda_k1, diff_lambda_q2, diff_lambda_k2, diff_subln, odd_w_in, odd_w_out, final_norm):
    batch, seq, d_model = x.shape
    depth = p.shape[0]
    assert seq % ATTN_TQ == 0 and (batch * seq) % TOKEN_TILE == 0 and seq % TOKEN_TILE == 0
    assert seq == C_PATTERNS[-1][0], "dilated kernel assumes the widest window spans the sequence"
    top_k = min(A_TOPK_MAX, seq // 4)
    cos_t, sin_t = _rope_tables(seq)
    gain = lambda g: g.reshape(1, -1).astype(F32)
    n_buckets = seq // ATTN_TQ
    r3 = lambda a: a.reshape(batch, seq, a.shape[-1])
    p_all = p.reshape(depth, batch * seq, p.shape[-1])
    f32 = lambda w: w.astype(F32)
    ffn_a = (f32(ffn_a_wg), f32(ffn_a_wu), f32(ffn_a_wd))
    ffn_b = (f32(ffn_b_wg), f32(ffn_b_wu), f32(ffn_b_wd))

    h = x.reshape(batch * seq, d_model)
    for i in range(depth):
        ga, gm = gain(norm_ffn_a[i]), gain(norm_mix[i])
        if i % 2 == 0:
            e = i // 2
            lambda_init = 0.8 - 0.6 * math.exp(-0.3 * i)
            (ha, qa, qi, qb, kb, ka, ki, vb, va_t, wi_t) = _pre_even(
                h, ga, gm, *_pack_even_w_in(even_w_in[e]), cos_t, sin_t, *ffn_a, layer=i, batch=batch, seq=seq)
            ka, ki, vb = r3(ka), r3(ki), r3(vb)
            lam_vecs = jnp.stack([diff_lambda_q1[e], diff_lambda_k1[e],
                                  diff_lambda_q2[e], diff_lambda_k2[e]]).astype(F32)
            out_a = [_dsa_bucket(qa, qi, wi_t, ka, va_t, ki, j=j, top_k=top_k) for j in range(n_buckets)]
            out_b = _diff_attention(lam_vecs, gain(diff_subln[e]), qb, kb, vb, lambda_init=lambda_init)
            mixes = [jnp.concatenate(out_a, axis=2), out_b.reshape(batch * seq, -1)]
            mix_transposed = (True, False)
            w_out_all, mixer_layer = f32(even_w_out), e
        else:
            o = i // 2
            ha, q, k, v = _pre_odd(h, ga, gm, cos_t, sin_t, *ffn_a, f32(odd_w_in),
                                   layer=i, mixer_layer=o, seq=seq)
            mixes = [_dilated(r3(q), r3(k), r3(v)).reshape(batch * seq, -1)]
            mix_transposed = (False,)
            w_out_all, mixer_layer = f32(odd_w_out), o
        gf = gain(final_norm) if i == depth - 1 else None
        h = _post(ha, mixes, gain(norm_ffn_b[i]), gain(norm_ple[i]), p_all, gf,
                  w_out_all, *ffn_b, f32(ple_gate), f32(ple_proj),
                  mix_transposed=mix_transposed, layer=i, mixer_layer=mixer_layer, seq=seq)
    return h.reshape(batch, seq, d_model)
```

```python
import functools
import math

import jax
import jax.numpy as jnp
import numpy as np
from jax import lax
from jax.experimental import pallas as pl
from jax.experimental.pallas import tpu as pltpu

F32 = jnp.float32
BF16 = jnp.bfloat16
I32 = jnp.int32
I16 = jnp.int16

HEAD_DIM = 64
ROPE_THETA = 10000.0
NORM_EPS = 1e-6
SUBLN_EPS = 1e-5
NEG_INF = -1e30
A_HEADS = 8
A_TOPK_MAX = 256
IDX_HEADS = 8
B_HEADS = 4
B_VDIM = 2 * HEAD_DIM
C_PATTERNS = ((128, 1), (512, 4), (2048, 16))
DIL_SPAN = 128

LANES = 128
BF16_ROWS = 16
VMEM_LIMIT_BYTES = 60000 * 1024
TOKEN_TILE = 512
ATTN_TQ = 256
DIL_MAX_UNROLL = 16
WEIGHT_STAGE_ROWS = 64
WEIGHT_STAGE_SLOTS = 8
DSA_HEAD_GROUP = 8
DIFF_HEAD_GROUP = 4

ATTN_Q_SCALE = (HEAD_DIM ** -0.5) * math.log2(math.e)

INT_MIN = -(2 ** 31)
KEY_NEG_INF = INT_MIN + 0x7FFFFF
KEY_POS_INF = 0x7F800000
IDX_BIG = 2 ** 30

EVEN_GROUP_COLS = A_HEADS * HEAD_DIM
assert EVEN_GROUP_COLS == IDX_HEADS * HEAD_DIM == 2 * B_HEADS * HEAD_DIM
EVEN_ROPE_COLS = 4 * EVEN_GROUP_COLS + 2 * HEAD_DIM
EVEN_T_ROWS = HEAD_DIM + IDX_HEADS

_CONTRACT_LAST = (((1,), (1,)), ((), ()))
_CONTRACT_FIRST = (((0,), (0,)), ((), ()))


def _cparams(n_grid):
    return pltpu.CompilerParams(dimension_semantics=("arbitrary",) * n_grid,
                                vmem_limit_bytes=VMEM_LIMIT_BYTES)


def _resident(shape):
    nd = len(shape)
    return pl.BlockSpec(shape, lambda *_: (0,) * nd, pipeline_mode=pl.Buffered(1))


_HBM = pl.BlockSpec(memory_space=pl.ANY)


def _weight_scratch(shapes):
    ns = WEIGHT_STAGE_SLOTS
    assert all(r % WEIGHT_STAGE_ROWS == 0 for r, _ in shapes)
    stage = (ns, WEIGHT_STAGE_ROWS, max(c for _, c in shapes))
    return [pltpu.VMEM(s, BF16) for s in shapes] + [pltpu.VMEM(stage, F32), pltpu.SemaphoreType.DMA((ns,))]


def _fetch_weights(layer, hbm_refs, vmem_refs, stage, sem):
    @pl.when(pl.program_id(0) == 0)
    def _():
        for w_hbm, w_vmem in zip(hbm_refs, vmem_refs):
            rows, cols = w_vmem.shape
            rc = WEIGHT_STAGE_ROWS
            n = rows // rc
            ns = stage.shape[0]
            ahead = min(ns - 1, n)

            def copy(c, w_hbm=w_hbm, rc=rc, cols=cols):
                slot = c % ns
                return pltpu.make_async_copy(w_hbm.at[layer, pl.ds(c * rc, rc), :],
                                             stage.at[slot, pl.ds(0, rc), pl.ds(0, cols)], sem.at[slot])

            def body(c, carry, copy=copy, w_vmem=w_vmem, rc=rc, cols=cols, n=n, ahead=ahead):
                @pl.when(c + ahead < n)
                def _():
                    copy(c + ahead).start()
                copy(c).wait()
                w_vmem[pl.ds(pl.multiple_of(c * rc, rc), rc), :] = stage[c % ns, 0:rc, 0:cols].astype(BF16)
                return carry

            for c in range(ahead):
                copy(c).start()
            lax.fori_loop(0, n, body, 0)


def _rms(x, g, eps):
    ms = jnp.mean(x * x, axis=-1, keepdims=True)
    return x * lax.rsqrt(ms + eps) * g


def _ffn_half(h, g, wg_ref, wu_ref, wd_ref):
    xn = _rms(h, g, NORM_EPS).astype(BF16)
    a = jnp.dot(xn, wg_ref[...], preferred_element_type=F32)
    b = jnp.dot(xn, wu_ref[...], preferred_element_type=F32)
    mid = (a / (1.0 + jnp.exp(-a)) * b).astype(BF16)
    return h + 0.5 * jnp.dot(mid, wd_ref[...], preferred_element_type=F32)


def _rope(y, cos_t, sin_t):
    w = y.shape[1]
    n = w // LANES
    c = jnp.concatenate([cos_t] * n, axis=1) if n > 1 else cos_t
    s = jnp.concatenate([sin_t] * n, axis=1) if n > 1 else sin_t
    lane = lax.broadcasted_iota(I32, y.shape, 1)
    first_half = (lane & (HEAD_DIM // 2)) == 0
    partner = jnp.where(first_half,
                        pltpu.roll(y, w - HEAD_DIM // 2, 1),
                        pltpu.roll(y, HEAD_DIM // 2, 1))
    return y * c + partner * s


def _pre_even_kernel(h_ref, ga_ref, gm_ref, wr_ref, wt_ref, cos_ref, sin_ref, wg_hbm, wu_hbm, wd_hbm,
                     ha_ref, qa_ref, qi_ref, qb_ref, kb_ref, ka_ref, ki_ref, vb_ref, vat_ref, wit_ref,
                     wg_ref, wu_ref, wd_ref, stage, sem, *, layer, idx_scale):
    _fetch_weights(layer, (wg_hbm, wu_hbm, wd_hbm), (wg_ref, wu_ref, wd_ref), stage, sem)
    ha = _ffn_half(h_ref[...], ga_ref[...], wg_ref, wu_ref, wd_ref)
    ha_ref[...] = ha
    xn = _rms(ha, gm_ref[...], NORM_EPS).astype(BF16)
    y = jnp.dot(xn, wr_ref[...], preferred_element_type=F32)
    yr = _rope(y[:, :EVEN_ROPE_COLS], cos_ref[...], sin_ref[...])
    vb_ref[...] = y[:, EVEN_ROPE_COLS:EVEN_ROPE_COLS + B_HEADS * B_VDIM].astype(BF16)
    gw = EVEN_GROUP_COLS
    for hd in range(gw // HEAD_DIM):
        lo = hd * HEAD_DIM
        qa_ref[hd] = (yr[:, lo:lo + HEAD_DIM] * ATTN_Q_SCALE).astype(BF16)
        qi_ref[hd] = yr[:, gw + lo:gw + lo + HEAD_DIM].astype(BF16)
        qb_ref[hd] = (yr[:, 2 * gw + lo:2 * gw + lo + HEAD_DIM] * ATTN_Q_SCALE).astype(BF16)
        kb_ref[hd] = yr[:, 3 * gw + lo:3 * gw + lo + HEAD_DIM].astype(BF16)
    ka_ref[...] = yr[:, 4 * gw:4 * gw + HEAD_DIM].astype(BF16)
    ki_ref[...] = yr[:, 4 * gw + HEAD_DIM:4 * gw + 2 * HEAD_DIM].astype(BF16)
    yt = lax.dot_general(wt_ref[...], xn, _CONTRACT_LAST, preferred_element_type=F32)
    vat_ref[...] = yt[0:HEAD_DIM].astype(BF16)
    wit_ref[...] = yt[HEAD_DIM:HEAD_DIM + IDX_HEADS] * idx_scale


def _pre_odd_kernel(h_ref, ga_ref, gm_ref, cos_ref, sin_ref, wg_hbm, wu_hbm, wd_hbm, win_hbm,
                    ha_ref, q_ref, k_ref, v_ref,
                    wg_ref, wu_ref, wd_ref, win_ref, stage, sem, *, layer, mixer_layer):
    _fetch_weights(layer, (wg_hbm, wu_hbm, wd_hbm), (wg_ref, wu_ref, wd_ref), stage, sem)
    _fetch_weights(mixer_layer, (win_hbm,), (win_ref,), stage, sem)
    ha = _ffn_half(h_ref[...], ga_ref[...], wg_ref, wu_ref, wd_ref)
    ha_ref[...] = ha
    xn = _rms(ha, gm_ref[...], NORM_EPS).astype(BF16)
    y = jnp.dot(xn, win_ref[...], preferred_element_type=F32)
    d = q_ref.shape[1]
    yr = _rope(y[:, :2 * d], cos_ref[...], sin_ref[...])
    q_ref[...] = yr[:, :d] * ATTN_Q_SCALE
    k_ref[...] = yr[:, d:2 * d]
    v_ref[...] = y[:, 2 * d:3 * d]


def _token_specs(tm, seq_tiles):
    row = lambda w: pl.BlockSpec((tm, w), lambda i: (i, 0))
    rope = pl.BlockSpec((tm, LANES), lambda i: (i % seq_tiles, 0))
    return row, rope


def _pre_even(h, ga, gm, w_cols, w_t, cos_t, sin_t, wg_all, wu_all, wd_all, *, layer, batch, seq):
    m, d = h.shape
    tm = TOKEN_TILE
    nt = seq // tm
    row, rope = _token_specs(tm, nt)
    n_hm = EVEN_GROUP_COLS // HEAD_DIM
    hm = lambda: pl.BlockSpec((None, n_hm, tm, HEAD_DIM), lambda i: (i // nt, 0, i % nt, 0))
    hm_shape = jax.ShapeDtypeStruct((batch, n_hm, seq, HEAD_DIM), BF16)
    tr = lambda rows: pl.BlockSpec((None, rows, tm), lambda i: (i // nt, 0, i % nt))
    idx_scale = (HEAD_DIM ** -0.5) * (IDX_HEADS ** -0.5)
    return pl.pallas_call(
        functools.partial(_pre_even_kernel, layer=layer, idx_scale=idx_scale),
        grid=(m // tm,),
        in_specs=[row(d), _resident(ga.shape), _resident(gm.shape), _resident(w_cols.shape),
                  _resident(w_t.shape), rope, rope, _HBM, _HBM, _HBM],
        out_specs=[row(d), hm(), hm(), hm(), hm(), row(HEAD_DIM), row(HEAD_DIM),
                   row(B_HEADS * B_VDIM), tr(HEAD_DIM), tr(IDX_HEADS)],
        out_shape=[jax.ShapeDtypeStruct((m, d), F32), hm_shape, hm_shape, hm_shape, hm_shape,
                   jax.ShapeDtypeStruct((m, HEAD_DIM), BF16), jax.ShapeDtypeStruct((m, HEAD_DIM), BF16),
                   jax.ShapeDtypeStruct((m, B_HEADS * B_VDIM), BF16),
                   jax.ShapeDtypeStruct((batch, HEAD_DIM, seq), BF16),
                   jax.ShapeDtypeStruct((batch, IDX_HEADS, seq), F32)],
        scratch_shapes=_weight_scratch([wg_all.shape[1:], wu_all.shape[1:], wd_all.shape[1:]]),
        compiler_params=_cparams(1),
        name="pre_even",
    )(h, ga, gm, w_cols, w_t, cos_t, sin_t, wg_all, wu_all, wd_all)


def _pre_odd(h, ga, gm, cos_t, sin_t, wg_all, wu_all, wd_all, win_all, *, layer, mixer_layer, seq):
    m, d = h.shape
    tm = TOKEN_TILE
    nt = seq // tm
    row, rope = _token_specs(tm, nt)
    out = jax.ShapeDtypeStruct((m, d), F32)
    return pl.pallas_call(
        functools.partial(_pre_odd_kernel, layer=layer, mixer_layer=mixer_layer),
        grid=(m // tm,),
        in_specs=[row(d), _resident(ga.shape), _resident(gm.shape), rope, rope, _HBM, _HBM, _HBM, _HBM],
        out_specs=[row(d), row(d), row(d), row(d)],
        out_shape=[out, out, out, out],
        scratch_shapes=_weight_scratch([wg_all.shape[1:], wu_all.shape[1:], wd_all.shape[1:], win_all.shape[1:]]),
        compiler_params=_cparams(1),
        name="pre_odd",
    )(h, ga, gm, cos_t, sin_t, wg_all, wu_all, wd_all, win_all)


def _post_kernel(*refs, n_mix, mix_transposed, final, layer, mixer_layer):
    refs = list(refs)
    h_ref = refs.pop(0)
    mix_refs = [refs.pop(0) for _ in range(n_mix)]
    gb_ref, gp_ref, p_ref = (refs.pop(0) for _ in range(3))
    gf_ref = refs.pop(0) if final else None
    wout_hbm, wg_hbm, wu_hbm, wd_hbm, pg_hbm, pp_hbm = (refs.pop(0) for _ in range(6))
    o_ref = refs.pop(0)
    wout_ref, wg_ref, wu_ref, wd_ref, pg_ref, pp_ref, stage, sem = refs
    _fetch_weights(mixer_layer, (wout_hbm,), (wout_ref,), stage, sem)
    _fetch_weights(layer, (wg_hbm, wu_hbm, wd_hbm, pg_hbm, pp_hbm), (wg_ref, wu_ref, wd_ref, pg_ref, pp_ref),
                   stage, sem)
    h1 = h_ref[...]
    row0 = 0
    for r, transposed in zip(mix_refs, mix_transposed):
        if transposed:
            w = wout_ref[row0:row0 + r.shape[0], :]
            h1 = h1 + lax.dot_general(r[...], w, _CONTRACT_FIRST, preferred_element_type=F32)
            row0 += r.shape[0]
        else:
            w = wout_ref[row0:row0 + r.shape[1], :]
            h1 = h1 + jnp.dot(r[...], w, preferred_element_type=F32)
            row0 += r.shape[1]
    h2 = _ffn_half(h1, gb_ref[...], wg_ref, wu_ref, wd_ref)
    z = jnp.dot(_rms(h2, gp_ref[...], NORM_EPS).astype(BF16), pg_ref[...], preferred_element_type=F32)
    gate = 1.0 / (1.0 + jnp.exp(-z))
    emb = jnp.dot(p_ref[...].astype(BF16), pp_ref[...], preferred_element_type=F32)
    h3 = h2 + gate * emb
    if final:
        h3 = _rms(h3, gf_ref[...], NORM_EPS)
    o_ref[...] = h3


def _post(h, mixes, gb, gp, p_all, gf, wout_all, wg_all, wu_all, wd_all, pg_all, pp_all,
          *, mix_transposed, layer, mixer_layer, seq):
    m, d = h.shape
    tm = TOKEN_TILE
    nt = seq // tm
    row = lambda w: pl.BlockSpec((tm, w), lambda i: (i, 0))
    mix_specs = [pl.BlockSpec((None, x.shape[1], tm), lambda i: (i // nt, 0, i % nt)) if t else row(x.shape[1])
                 for x, t in zip(mixes, mix_transposed)]
    final = gf is not None
    weights = [wout_all, wg_all, wu_all, wd_all, pg_all, pp_all]
    args = [h, *mixes, gb, gp, p_all] + ([gf] if final else []) + weights
    specs = ([row(d), *mix_specs, _resident(gb.shape), _resident(gp.shape),
              pl.BlockSpec((None, tm, p_all.shape[2]), lambda i: (layer, i, 0))]
             + ([_resident(gf.shape)] if final else []) + [_HBM] * len(weights))
    return pl.pallas_call(
        functools.partial(_post_kernel, n_mix=len(mixes), mix_transposed=mix_transposed, final=final,
                          layer=layer, mixer_layer=mixer_layer),
        grid=(m // tm,),
        in_specs=specs,
        out_specs=row(d),
        out_shape=jax.ShapeDtypeStruct((m, d), F32),
        scratch_shapes=_weight_scratch([w.shape[1:] for w in weights]),
        compiler_params=_cparams(1),
        name="post",
    )(*args)


def _key_count(mask):
    return jnp.sum(jnp.where(mask, 1.0, 0.0), axis=0, keepdims=True)


def _key_to_float(key):
    key = jnp.clip(key, jnp.int32(KEY_NEG_INF), jnp.int32(KEY_POS_INF))
    return pltpu.bitcast(jnp.where(key < 0, key ^ jnp.int32(0x7FFFFFFF), key), F32)


def _count16(mask):
    s = mask.shape[0]
    ones = jnp.where(mask, jnp.ones((), I16), jnp.zeros((), I16))
    parts = [ones[i:i + BF16_ROWS] for i in range(0, s, BF16_ROWS)]
    while len(parts) > 1:
        nxt = [parts[i] + parts[i + 1] for i in range(0, len(parts) - 1, 2)]
        parts = nxt + ([parts[-1]] if len(parts) % 2 else [])
    return jnp.sum(parts[0].astype(F32), axis=0, keepdims=True)


def _kth_largest_fast(val_scr, hi_scr, lo_scr, *, top_k):
    tq = val_scr.shape[1]
    half_min = jnp.int32(-(2 ** 15))

    def keys():
        bits = pltpu.bitcast(val_scr[...], I32)
        return jnp.where(bits < 0, bits ^ jnp.int32(0x7FFFFFFF), bits)

    hi_scr[...] = lax.shift_right_arithmetic(keys(), jnp.int32(16)).astype(I16)

    def search(ref, need):
        def step(it, acc):
            cand = acc + lax.shift_left(jnp.int32(1), jnp.int32(15) - it)
            cnt = _count16(ref[...] >= cand.astype(I16))
            return jnp.where(cnt >= need, cand, acc)
        return lax.fori_loop(0, 16, step, jnp.full((1, tq), half_min, I32))

    hi = search(hi_scr, top_k)
    hi16 = hi.astype(I16)
    need = top_k - _count16(hi_scr[...] > hi16)
    low = ((keys() & jnp.int32(0xFFFF)) + half_min).astype(I16)
    lo_scr[...] = jnp.where(hi_scr[...] == hi16, low, half_min.astype(I16))
    lo = search(lo_scr, need)
    return _key_to_float(lax.shift_left(hi, jnp.int32(16)) | ((lo - half_min) & jnp.int32(0xFFFF)))


def _kth_largest_float(val_scr, *, top_k):
    tq = val_scr.shape[1]

    def value_step(it, lo):
        cand = lo ^ lax.shift_left(jnp.int32(1), jnp.int32(31) - it)
        cnt = _key_count(val_scr[...] >= _key_to_float(cand))
        return jnp.where(cnt >= top_k, cand, lo)

    return _key_to_float(lax.fori_loop(0, 32, value_step, jnp.full((1, tq), INT_MIN, I32)))


def _topk_bias(score, causal, val_scr, idx_scr, hi_scr, lo_scr, thr_scr, *, top_k):
    s, tq = score.shape
    key_idx = lax.broadcasted_iota(I32, (s, tq), 0)
    val_scr[...] = jnp.where(causal, score, -jnp.inf)

    thr_scr[...] = jnp.broadcast_to(_kth_largest_fast(val_scr, hi_scr, lo_scr, top_k=top_k), thr_scr.shape)
    thr0 = thr_scr[0:1, :]
    verified = jnp.logical_and(jnp.min(_key_count(val_scr[...] >= thr0)) >= top_k,
                               jnp.max(_key_count(val_scr[...] > thr0)) < top_k)

    @pl.when(jnp.logical_not(verified))
    def _():
        thr_scr[...] = jnp.broadcast_to(_kth_largest_float(val_scr, top_k=top_k), thr_scr.shape)

    thr = thr_scr[0:1, :]
    n_ge = _key_count(val_scr[...] >= thr)
    excess = jnp.max(n_ge) > top_k

    @pl.when(jnp.logical_not(excess))
    def _():
        val_scr[...] = jnp.where(val_scr[...] >= thr, 0.0, NEG_INF)

    @pl.when(excess)
    def _():
        vals = val_scr[...]
        above = vals > thr
        need = top_k - _key_count(above)
        idx_scr[...] = jnp.where(vals == thr, key_idx, jnp.int32(IDX_BIG))
        n_bits = max(1, int(math.ceil(math.log2(s))))

        def index_step(it, p):
            cand = p + lax.shift_left(jnp.int32(1), jnp.int32(n_bits - 1) - it)
            cnt = _key_count(idx_scr[...] < cand)
            return jnp.where(cnt < need, cand, p)

        p = lax.fori_loop(0, n_bits, index_step, jnp.zeros((1, tq), I32))
        keep_tie = idx_scr[...] <= p
        keep = jnp.where(above, 0.0, jnp.where(keep_tie, 0.0, NEG_INF))
        val_scr[...] = jnp.where(causal, keep, NEG_INF)

    return val_scr[...]


def _with_ones_rows(v_t):
    return jnp.concatenate([v_t, jnp.ones((BF16_ROWS, v_t.shape[1]), BF16)], axis=0)


def _dsa_kernel(qa_ref, qi_ref, wit_ref, ka_ref, vat_ref, ki_ref, o_ref, *scratch, s_len, q_start, top_k):
    tq = o_ref.shape[1]
    key_idx = lax.broadcasted_iota(I32, (s_len, tq), 0)
    q_pos = lax.broadcasted_iota(I32, (s_len, tq), 1) + q_start
    causal = key_idx <= q_pos
    if s_len > top_k:
        ki = ki_ref[...]
        w_t = wit_ref[...]
        score = jnp.zeros((s_len, tq), F32)
        g = DSA_HEAD_GROUP
        for h0 in range(0, IDX_HEADS, g):
            dots = lax.dot_general(ki, qi_ref[h0:h0 + g].reshape(g * tq, HEAD_DIM), _CONTRACT_LAST,
                                   preferred_element_type=F32)
            for i in range(g):
                score = score + w_t[h0 + i:h0 + i + 1, :] * jnp.maximum(dots[:, i * tq:(i + 1) * tq], 0.0)
        bias = _topk_bias(score, causal, *scratch, top_k=top_k)
    else:
        bias = jnp.where(causal, 0.0, NEG_INF)
    ka = ka_ref[...]
    v_ext = _with_ones_rows(vat_ref[...])
    g = DSA_HEAD_GROUP
    bias_g = jnp.concatenate([bias] * g, axis=1)
    for h0 in range(0, A_HEADS, g):
        q_g = qa_ref[h0:h0 + g].reshape(g * tq, HEAD_DIM)
        s_t = lax.dot_general(ka, q_g, _CONTRACT_LAST, preferred_element_type=F32) + bias_g
        m = jnp.max(s_t, axis=0, keepdims=True)
        p = jnp.exp2(s_t - m).astype(BF16)
        r = jnp.dot(v_ext, p, preferred_element_type=F32)
        o = (r[0:HEAD_DIM] * (1.0 / r[HEAD_DIM:HEAD_DIM + 1])).astype(BF16)
        for i in range(g):
            o_ref[(h0 + i) * HEAD_DIM:(h0 + i + 1) * HEAD_DIM, :] = o[:, i * tq:(i + 1) * tq]


def _dsa_bucket(qa, qi, wi_t, ka, va_t, ki, *, j, top_k):
    batch, _, seq, _ = qa.shape
    tq = ATTN_TQ
    s_len = (j + 1) * tq
    hm = pl.BlockSpec((None, A_HEADS, tq, HEAD_DIM), lambda b: (b, 0, j, 0))
    kv = pl.BlockSpec((None, s_len, HEAD_DIM), lambda b: (b, 0, 0))
    scratch = [pltpu.VMEM((s_len, tq), F32), pltpu.VMEM((s_len, tq), I32), pltpu.VMEM((s_len, tq), I16),
               pltpu.VMEM((s_len, tq), I16), pltpu.VMEM((8, tq), F32)] if s_len > top_k else []
    return pl.pallas_call(
        functools.partial(_dsa_kernel, s_len=s_len, q_start=j * tq, top_k=top_k),
        grid=(batch,),
        in_specs=[hm, hm, pl.BlockSpec((None, IDX_HEADS, tq), lambda b: (b, 0, j)), kv,
                  pl.BlockSpec((None, HEAD_DIM, s_len), lambda b: (b, 0, 0)), kv],
        out_specs=pl.BlockSpec((None, A_HEADS * HEAD_DIM, tq), lambda b: (b, 0, 0)),
        out_shape=jax.ShapeDtypeStruct((batch, A_HEADS * HEAD_DIM, tq), BF16),
        scratch_shapes=scratch,
        compiler_params=_cparams(1),
        name=f"dsa_{j}",
    )(qa, qi, wi_t, ka, va_t, ki)


def _causal_attend(q, k_ref, v_ext, c, j, tri_bias):
    tq = ATTN_TQ
    off = j * tq
    s_d = lax.dot_general(q, k_ref[c, off:off + tq, :], _CONTRACT_LAST, preferred_element_type=F32) + tri_bias
    m = jnp.max(s_d, axis=1, keepdims=True)
    if j > 0:
        s_o = lax.dot_general(q, k_ref[c, 0:off, :], _CONTRACT_LAST, preferred_element_type=F32)
        m = jnp.maximum(m, jnp.max(s_o, axis=1, keepdims=True))
    r = jnp.dot(jnp.exp2(s_d - m).astype(BF16), v_ext[off:off + tq, :], preferred_element_type=F32)
    if j > 0:
        r = r + jnp.dot(jnp.exp2(s_o - m).astype(BF16), v_ext[0:off, :], preferred_element_type=F32)
    return r[:, :B_VDIM] * (1.0 / r[:, B_VDIM:])


def _diff_kernel(lam_ref, subln_ref, q_ref, k_ref, v_ref, o_ref, *, lambda_init):
    seq = o_ref.shape[0]
    tq = ATTN_TQ
    lv = lam_ref[...]
    lam = (jnp.exp(jnp.sum(lv[0:1] * lv[1:2], axis=1, keepdims=True))
           - jnp.exp(jnp.sum(lv[2:3] * lv[3:4], axis=1, keepdims=True)) + lambda_init)
    row = lax.broadcasted_iota(I32, (tq, tq), 0)
    col = lax.broadcasted_iota(I32, (tq, tq), 1)
    tri_bias = jnp.where(col <= row, 0.0, NEG_INF)
    subln = subln_ref[...]
    n_heads = q_ref.shape[0] // 2
    v_exts = []
    for hd in range(n_heads):
        v = v_ref[:, hd * B_VDIM:(hd + 1) * B_VDIM]
        v_exts.append(jnp.concatenate([v, jnp.ones(v.shape, BF16)], axis=1))
    for j in range(seq // tq):
        rows = slice(j * tq, (j + 1) * tq)
        for hd in range(n_heads):
            a = (_causal_attend(q_ref[2 * hd, rows, :], k_ref, v_exts[hd], 2 * hd, j, tri_bias)
                 - lam * _causal_attend(q_ref[2 * hd + 1, rows, :], k_ref, v_exts[hd], 2 * hd + 1, j, tri_bias))
            y = _rms(a, subln, SUBLN_EPS) * (1.0 - lambda_init)
            o_ref[rows, hd * B_VDIM:(hd + 1) * B_VDIM] = y.astype(BF16)


def _diff_attention(lam_vecs, subln, qb, kb, vb, *, lambda_init):
    batch, _, seq, _ = qb.shape
    hg = DIFF_HEAD_GROUP
    qk = lambda: pl.BlockSpec((None, 2 * hg, seq, HEAD_DIM), lambda b, h: (b, h, 0, 0))
    vo = lambda: pl.BlockSpec((None, seq, hg * B_VDIM), lambda b, h: (b, 0, h))
    return pl.pallas_call(
        functools.partial(_diff_kernel, lambda_init=lambda_init),
        grid=(batch, B_HEADS // hg),
        in_specs=[pl.BlockSpec(lam_vecs.shape, lambda b, h: (0, 0)),
                  pl.BlockSpec(subln.shape, lambda b, h: (0, 0)),
                  qk(), qk(), vo()],
        out_specs=vo(),
        out_shape=jax.ShapeDtypeStruct((batch, seq, B_HEADS * B_VDIM), BF16),
        compiler_params=_cparams(2),
        name="diff",
    )(lam_vecs, subln, qb, kb, vb)


def _unroll_for(trips):
    return max(u for u in range(1, DIL_MAX_UNROLL + 1) if trips % u == 0)


def _dil_kernel(q_ref, k_ref, v_ref, o_ref, m_scr, l_scr, acc_scr, bias_scr):
    seq = q_ref.shape[0]
    span = DIL_SPAN
    lane = lax.broadcasted_iota(I32, (span, LANES), 1)
    head0 = lane < HEAD_DIM

    qi = lax.broadcasted_iota(I32, (2 * span, 2 * span), 0) & (span - 1)
    kj = lax.broadcasted_iota(I32, (2 * span, 2 * span), 1)
    dist = kj - qi
    bias_scr[...] = jnp.where(jnp.where(kj < span, dist, span - dist) >= 0, 0.0, NEG_INF)

    def rows_at(start, d):
        return pl.ds(start, span) if d == 1 else pl.ds(start, span, stride=d)

    def tile(t, *, d, blk0, first):
        r = t % d
        blk = blk0 + t // d
        start = r + blk * (span * d)
        if d == 1:
            start = pl.multiple_of(start, span)
        rows = rows_at(start, d)
        qb = q_ref[rows, :]
        qs = jnp.concatenate([jnp.where(head0, qb, 0.0), jnp.where(head0, 0.0, qb)], axis=0).astype(BF16)
        kk = k_ref[rows, :]
        vv = v_ref[rows, :]
        has_prev = blk0 > 0
        if has_prev:
            prev = rows_at(start - span * d, d)
            kk = jnp.concatenate([k_ref[prev, :], kk], axis=0)
            vv = jnp.concatenate([v_ref[prev, :], vv], axis=0)
            bias = bias_scr[...]
        else:
            bias = bias_scr[:, span:]
        s = lax.dot_general(qs, kk.astype(BF16), _CONTRACT_LAST, preferred_element_type=F32) + bias
        m = jnp.max(s, axis=1, keepdims=True)
        p = jnp.exp2(s - m).astype(BF16)
        v_ext = jnp.concatenate([vv.astype(BF16), jnp.ones(vv.shape, BF16)], axis=1)
        pv = jnp.dot(p, v_ext, preferred_element_type=F32)
        acc_t = jnp.where(head0, pv[:span, :LANES], pv[span:, :LANES])
        l_t = jnp.where(head0, pv[:span, LANES:], pv[span:, LANES:])
        m_t = jnp.where(head0, m[:span], m[span:])
        if first:
            m_scr[rows, :] = m_t
            l_scr[rows, :] = l_t
            acc_scr[rows, :] = acc_t
        else:
            m_o = m_scr[rows, :]
            m_n = jnp.maximum(m_o, m_t)
            ea = jnp.exp2(m_o - m_n)
            eb = jnp.exp2(m_t - m_n)
            l_scr[rows, :] = ea * l_scr[rows, :] + eb * l_t
            acc_scr[rows, :] = ea * acc_scr[rows, :] + eb * acc_t
            m_scr[rows, :] = m_n

    for pi, (window, d) in enumerate(reversed(C_PATTERNS)):
        n_blocks = (seq // d) // span
        first = pi == 0

        def no_prev(t, carry, d=d, first=first):
            tile(t, d=d, blk0=0, first=first)
            return carry

        def with_prev(t, carry, d=d, first=first):
            tile(t, d=d, blk0=1, first=first)
            return carry

        lax.fori_loop(0, d, no_prev, 0, unroll=_unroll_for(d))
        if n_blocks > 1:
            trips = d * (n_blocks - 1)
            lax.fori_loop(0, trips, with_prev, 0, unroll=_unroll_for(trips))

    o_ref[...] = (acc_scr[...] * (1.0 / l_scr[...])).astype(BF16)


def _dilated(q, k, v):
    batch, seq, d = q.shape
    n_pairs = d // LANES
    blk = lambda: pl.BlockSpec((None, seq, LANES), lambda b, hp: (b, 0, hp))
    return pl.pallas_call(
        _dil_kernel,
        grid=(batch, n_pairs),
        in_specs=[blk(), blk(), blk()],
        out_specs=blk(),
        out_shape=jax.ShapeDtypeStruct((batch, seq, d), BF16),
        scratch_shapes=[pltpu.VMEM((seq, LANES), F32)] * 3 + [pltpu.VMEM((2 * DIL_SPAN, 2 * DIL_SPAN), F32)],
        compiler_params=_cparams(2),
        name="dilated",
    )(q, k, v)


def _rope_tables(seq):
    inv = 1.0 / (ROPE_THETA ** (jnp.arange(0, HEAD_DIM, 2, dtype=F32) / HEAD_DIM))
    ang = jnp.arange(seq, dtype=F32)[:, None] * inv[None, :]
    cos, sin = jnp.cos(ang), jnp.sin(ang)
    cos_t = jnp.concatenate([cos, cos, cos, cos], axis=1)
    sin_t = jnp.concatenate([-sin, sin, -sin, sin], axis=1)
    return cos_t, sin_t


def _pack_even_w_in(w):
    gw = EVEN_GROUP_COLS
    cuts = np.cumsum([0, gw, HEAD_DIM, HEAD_DIM, gw, HEAD_DIM, IDX_HEADS, gw, gw, B_HEADS * B_VDIM])
    qa, ka, va, qi, ki, wi, qb, kb, vb = [w[:, cuts[i]:cuts[i + 1]] for i in range(9)]
    w_cols = jnp.concatenate([qa, qi, qb, kb, ka, ki, vb], axis=1).astype(BF16)
    w_t = jnp.concatenate([va, wi], axis=1).T
    pad = (-w_t.shape[0]) % BF16_ROWS
    w_t = jnp.pad(w_t, ((0, pad), (0, 0))).astype(BF16)
    assert w_cols.shape[1] == EVEN_ROPE_COLS + B_HEADS * B_VDIM and w_t.shape[0] >= EVEN_T_ROWS
    return w_cols, w_t


def kernel(x, p, norm_ffn_a, ffn_a_wg, ffn_a_wu, ffn_a_wd, norm_mix, norm_ffn_b, ffn_b_wg, ffn_b_wu,
           ffn_b_wd, norm_ple, ple_gate, ple_proj, even_w_in, even_w_out, diff_lambda_q1,
           diff_lambda_k1, diff_lambda_q2, diff_lambda_k2, diff_subln, odd_w_in, odd_w_out, final_norm):
    batch, seq, d_model = x.shape
    depth = p.shape[0]
    assert seq % ATTN_TQ == 0 and (batch * seq) % TOKEN_TILE == 0 and seq % TOKEN_TILE == 0
    assert seq == C_PATTERNS[-1][0], "dilated kernel assumes the widest window spans the sequence"
    top_k = min(A_TOPK_MAX, seq // 4)
    cos_t, sin_t = _rope_tables(seq)
    gain = lambda g: g.reshape(1, -1).astype(F32)
    n_buckets = seq // ATTN_TQ
    r3 = lambda a: a.reshape(batch, seq, a.shape[-1])
    p_all = p.reshape(depth, batch * seq, p.shape[-1])
    f32 = lambda w: w.astype(F32)
    ffn_a = (f32(ffn_a_wg), f32(ffn_a_wu), f32(ffn_a_wd))
    ffn_b = (f32(ffn_b_wg), f32(ffn_b_wu), f32(ffn_b_wd))

    h = x.reshape(batch * seq, d_model)
    for i in range(depth):
        ga, gm = gain(norm_ffn_a[i]), gain(norm_mix[i])
        if i % 2 == 0:
            e = i // 2
            lambda_init = 0.8 - 0.6 * math.exp(-0.3 * i)
            (ha, qa, qi, qb, kb, ka, ki, vb, va_t, wi_t) = _pre_even(
                h, ga, gm, *_pack_even_w_in(even_w_in[e]), cos_t, sin_t, *ffn_a, layer=i, batch=batch, seq=seq)
            ka, ki, vb = r3(ka), r3(ki), r3(vb)
            lam_vecs = jnp.stack([diff_lambda_q1[e], diff_lambda_k1[e],
                                  diff_lambda_q2[e], diff_lambda_k2[e]]).astype(F32)
            out_a = [_dsa_bucket(qa, qi, wi_t, ka, va_t, ki, j=j, top_k=top_k) for j in range(n_buckets)]
            out_b = _diff_attention(lam_vecs, gain(diff_subln[e]), qb, kb, vb, lambda_init=lambda_init)
            mixes = [jnp.concatenate(out_a, axis=2), out_b.reshape(batch * seq, -1)]
            mix_transposed = (True, False)
            w_out_all, mixer_layer = f32(even_w_out), e
        else:
            o = i // 2
            ha, q, k, v = _pre_odd(h, ga, gm, cos_t, sin_t, *ffn_a, f32(odd_w_in),
                                   layer=i, mixer_layer=o, seq=seq)
            mixes = [_dilated(r3(q), r3(k), r3(v)).reshape(batch * seq, -1)]
            mix_transposed = (False,)
            w_out_all, mixer_layer = f32(odd_w_out), o
        gf = gain(final_norm) if i == depth - 1 else None
        h = _post(ha, mixes, gain(norm_ffn_b[i]), gain(norm_ple[i]), p_all, gf,
                  w_out_all, *ffn_b, f32(ple_gate), f32(ple_proj),
                  mix_transposed=mix_transposed, layer=i, mixer_layer=mixer_layer, seq=seq)
    return h.reshape(batch, seq, d_model)
```
